```python
import numpy as np
import jax
import jax.numpy as jnp
from jax import lax

D_MODEL = 1024
BATCH = 4
SEQ = 8192
DEPTH = 1

HEAD_DIM = 64
NSA_HEADS = 8
NSA_KV_HEADS = 2
NSA_GROUP = NSA_HEADS // NSA_KV_HEADS
MOBA_HEADS = 8
D_NSA = NSA_HEADS * HEAD_DIM
D_NSA_KV = NSA_KV_HEADS * HEAD_DIM
D_MOBA = MOBA_HEADS * HEAD_DIM
D_MIX = D_NSA + D_MOBA
N_NSA_BRANCHES = 3
ROPE_THETA = 500000.0
ROPE_DIM = HEAD_DIM // 4
CMP_BLOCK = 32
CMP_STRIDE = 16
CMP_HIDDEN = 4 * HEAD_DIM
SEL_BLOCK = 64
SEL_TOPK = 16
WINDOW = 512
NSA_Q_CHUNK = 64
MOBA_BLOCK = 256
MOBA_TOPK = 3
MOBA_Q_CHUNK = 32
LN_EPS = 1e-5
DEEPNORM_ALPHA = (2.0 * DEPTH) ** 0.25
DEEPNORM_BETA = (8.0 * DEPTH) ** -0.25
NEG_BIG = -1e30
POS_BIG = 1e30

PROJ_SIZES = (D_NSA, D_NSA_KV, D_NSA_KV, D_NSA_KV, D_NSA_KV, D_NSA_KV, D_NSA_KV,
              N_NSA_BRANCHES * NSA_HEADS, D_NSA, D_MOBA, D_MOBA, D_MOBA, D_MOBA)
VALUE_SLOTS = (2, 4, 6, 11)
D_PROJ = sum(PROJ_SIZES)
PROJ_OFFSETS = tuple(int(o) for o in np.cumsum(PROJ_SIZES)[:-1])

kernel_name = "hymba_nsa_moba_deepnorm_block"


def rope_tables(T):
    inv_freq = ROPE_THETA ** (-jnp.arange(0, ROPE_DIM, 2, dtype=jnp.float32) / ROPE_DIM)
    ang = jnp.arange(T, dtype=jnp.float32)[:, None] * inv_freq[None, :]
    return jnp.cos(ang), jnp.sin(ang)


def partial_rope(x, cos, sin):
    half = ROPE_DIM // 2
    xr = x[..., :ROPE_DIM].astype(jnp.float32)
    x1, x2 = xr[..., :half], xr[..., half:]
    c = cos[None, :, None, :]
    s = sin[None, :, None, :]
    rot = jnp.concatenate([x1 * c - x2 * s, x2 * c + x1 * s], axis=-1).astype(x.dtype)
    return jnp.concatenate([rot, x[..., ROPE_DIM:]], axis=-1)


def masked_softmax(s, mask):
    s = jnp.where(mask, s.astype(jnp.float32), NEG_BIG)
    m = jnp.max(s, axis=-1, keepdims=True)
    e = jnp.where(mask, jnp.exp(s - m), 0.0)
    d = jnp.sum(e, axis=-1, keepdims=True)
    return e / jnp.where(d > 0, d, 1.0)


def layer_norm(h, gain, bias):
    hf = h.astype(jnp.float32)
    mu = jnp.mean(hf, axis=-1, keepdims=True)
    var = jnp.mean(jnp.square(hf - mu), axis=-1, keepdims=True)
    y = (hf - mu) * lax.rsqrt(var + LN_EPS) * gain.astype(jnp.float32) + bias.astype(jnp.float32)
    return y.astype(h.dtype)


def gather_blocks(blocks, idx):
    return jax.vmap(jax.vmap(lambda bl, ix: bl[ix]))(blocks, idx)


def compress(kv, pos, w1, b1, w2):
    B, T, G, d = kv.shape
    n_per = CMP_BLOCK // CMP_STRIDE
    nc0 = T // CMP_STRIDE
    nc = nc0 - n_per + 1
    ch = kv.reshape(B, nc0, CMP_STRIDE, G, d)
    blocks = jnp.concatenate([ch[:, i:i + nc] for i in range(n_per)], axis=2)
    blocks = blocks + pos[None, None, :, None, :]
    flat = blocks.transpose(0, 3, 1, 2, 4).reshape(B, G, nc, CMP_BLOCK * d)
    return jax.nn.silu(flat @ w1 + b1) @ w2


def nsa_attention(q, k_cmp, v_cmp, k_slc, v_slc, k_win, v_win, gate_logits,
                  cmp_pos_k, cmp_w1_k, cmp_b1_k, cmp_w2_k,
                  cmp_pos_v, cmp_w1_v, cmp_b1_v, cmp_w2_v, cos, sin):
    B, T = q.shape[0], q.shape[1]
    G, R, d = NSA_KV_HEADS, NSA_GROUP, HEAD_DIM
    scale = d ** -0.5

    def to_groups(a):
        return a.reshape(B, T, G, R, d).transpose(0, 2, 3, 1, 4)

    q_raw = to_groups(q)
    q_rot = to_groups(partial_rope(q, cos, sin))

    kc = compress(k_cmp, cmp_pos_k, cmp_w1_k, cmp_b1_k, cmp_w2_k)
    vc = compress(v_cmp, cmp_pos_v, cmp_w1_v, cmp_b1_v, cmp_w2_v)
    nc = kc.shape[2]
    ns = T // SEL_BLOCK
    ks = partial_rope(k_slc, cos, sin).transpose(0, 2, 1, 3).reshape(B, G, ns, SEL_BLOCK, d)
    vs = v_slc.transpose(0, 2, 1, 3).reshape(B, G, ns, SEL_BLOCK, d)
    pad = ((0, 0), (0, 0), (WINDOW, 0), (0, 0))
    kw = jnp.pad(partial_rope(k_win, cos, sin).transpose(0, 2, 1, 3), pad)
    vw = jnp.pad(v_win.transpose(0, 2, 1, 3), pad)
    gates = jax.nn.sigmoid(gate_logits.astype(jnp.float32)).reshape(B, T, G, R, N_NSA_BRANCHES)
    gates = gates.transpose(0, 2, 3, 1, 4)

    c_start = jnp.arange(nc)[:, None] * CMP_STRIDE
    s_start = jnp.arange(ns)[None, :] * SEL_BLOCK
    overlap = ((c_start < s_start + SEL_BLOCK) & (c_start + CMP_BLOCK > s_start)).astype(jnp.float32)
    cmp_end = jnp.arange(nc) * CMP_STRIDE + CMP_BLOCK - 1
    n_sel = min(SEL_TOPK, ns)
    Qc = NSA_Q_CHUNK
    blk_ids = jnp.arange(ns)

    def chunk(c):
        s0 = c * Qc
        t = s0 + jnp.arange(Qc)
        qr = lax.dynamic_slice_in_dim(q_rot, s0, Qc, axis=3)
        qn = lax.dynamic_slice_in_dim(q_raw, s0, Qc, axis=3)
        g = lax.dynamic_slice_in_dim(gates, s0, Qc, axis=3)

        sc = jnp.einsum('bghqd,bgcd->bghqc', qn, kc) * scale
        pc = masked_softmax(sc, cmp_end[None, :] <= t[:, None])
        o_cmp = jnp.einsum('bghqc,bgcd->bghqd', pc.astype(vc.dtype), vc)

        imp = jnp.einsum('bghqc,cs->bgqs', pc, overlap)
        qblk = (t // SEL_BLOCK)[:, None]
        forced = (blk_ids[None, :] == 0) | (blk_ids[None, :] == qblk) | (blk_ids[None, :] == qblk - 1)
        imp = jnp.where(forced, POS_BIG, jnp.where(blk_ids[None, :] > qblk, NEG_BIG, imp))
        _, idx = lax.top_k(imp, n_sel)
        gk = gather_blocks(ks, idx)
        gv = gather_blocks(vs, idx)
        kpos = idx[..., None] * SEL_BLOCK + jnp.arange(SEL_BLOCK)
        ms = (kpos <= t[None, None, :, None, None]).reshape(B, G, 1, Qc, n_sel * SEL_BLOCK)
        ss = jnp.einsum('bghqd,bgqnld->bghqnl', qr, gk) * scale
        ps = masked_softmax(ss.reshape(B, G, R, Qc, n_sel * SEL_BLOCK), ms)
        ps = ps.reshape(ss.shape).astype(gv.dtype)
        o_slc = jnp.einsum('bghqnl,bgqnld->bghqd', ps, gv)

        kwin = lax.dynamic_slice_in_dim(kw, s0, Qc + WINDOW, axis=2)
        vwin = lax.dynamic_slice_in_dim(vw, s0, Qc + WINDOW, axis=2)
        kpos_w = (s0 - WINDOW + jnp.arange(Qc + WINDOW))[None, :]
        mw = (kpos_w <= t[:, None]) & (kpos_w > t[:, None] - WINDOW) & (kpos_w >= 0)
        sw = jnp.einsum('bghqd,bgkd->bghqk', qr, kwin) * scale
        pw = masked_softmax(sw, mw).astype(vwin.dtype)
        o_win = jnp.einsum('bghqk,bgkd->bghqd', pw, vwin)

        o = g[..., 0:1] * o_cmp + g[..., 1:2] * o_slc + g[..., 2:3] * o_win
        return o.astype(q.dtype)

    out = lax.map(chunk, jnp.arange(T // Qc))
    return out.transpose(1, 0, 4, 2, 3, 5).reshape(B, T, D_NSA)


def moba_attention(q, k, v, cos, sin):
    B, T = q.shape[0], q.shape[1]
    H, d, MB = MOBA_HEADS, HEAD_DIM, MOBA_BLOCK
    scale = d ** -0.5
    nb = max(-(-T // MB), 2)
    Tp = nb * MB
    pad = ((0, 0), (0, Tp - T), (0, 0), (0, 0))
    qp = jnp.pad(partial_rope(q, cos, sin), pad).transpose(0, 2, 1, 3)
    kp = jnp.pad(partial_rope(k, cos, sin), pad).transpose(0, 2, 1, 3)
    vp = jnp.pad(v, pad).transpose(0, 2, 1, 3)
    kb = kp.reshape(B, H, nb, MB, d)
    vb = vp.reshape(B, H, nb, MB, d)
    kmean = jnp.mean(kb.astype(jnp.float32), axis=3)
    k_top = min(MOBA_TOPK, nb - 1)
    Qc = MOBA_Q_CHUNK
    blk_ids = jnp.arange(nb)

    def chunk(c):
        s0 = c * Qc
        own = s0 // MB
        t = s0 + jnp.arange(Qc)
        qc = lax.dynamic_slice_in_dim(qp, s0, Qc, axis=2)
        sg = jnp.einsum('bhqd,bhnd->bhqn', qc.astype(jnp.float32), kmean)
        sg = jnp.where(blk_ids < own, sg, NEG_BIG)
        _, idx = lax.top_k(sg, k_top)
        valid = idx < own
        gk = gather_blocks(kb, idx)
        gv = gather_blocks(vb, idx)
        sp = jnp.einsum('bhqd,bhqnld->bhqnl', qc, gk).reshape(B, H, Qc, k_top * MB)
        mp = jnp.broadcast_to(valid[..., None], (B, H, Qc, k_top, MB)).reshape(B, H, Qc, k_top * MB)
        ko = lax.dynamic_slice_in_dim(kp, own * MB, MB, axis=2)
        vo = lax.dynamic_slice_in_dim(vp, own * MB, MB, axis=2)
        so = jnp.einsum('bhqd,bhld->bhql', qc, ko)
        mo = jnp.broadcast_to((own * MB + jnp.arange(MB))[None, :] <= t[:, None], so.shape)
        s = jnp.concatenate([sp, so], axis=-1) * scale
        m = jnp.concatenate([mp, mo], axis=-1)
        p = masked_softmax(s, m).astype(v.dtype)
        pp = p[..., :k_top * MB].reshape(B, H, Qc, k_top, MB)
        o = jnp.einsum('bhqnl,bhqnld->bhqd', pp, gv) + jnp.einsum('bhql,bhld->bhqd', p[..., k_top * MB:], vo)
        return o.astype(q.dtype)

    out = lax.map(chunk, jnp.arange(Tp // Qc))
    return out.transpose(1, 0, 3, 2, 4).reshape(B, Tp, D_MOBA)[:, :T]


def hybrid_layer(x, w_in, cmp_pos_k, cmp_w1_k, cmp_b1_k, cmp_w2_k,
                 cmp_pos_v, cmp_w1_v, cmp_b1_v, cmp_w2_v, w_out, ln_gain, ln_bias, cos, sin):
    B, T, _ = x.shape
    proj = x @ w_in
    (nsa_q, nsa_kc, nsa_vc, nsa_ks, nsa_vs, nsa_kw, nsa_vw, nsa_g, nsa_z,
     moba_q, moba_k, moba_v, moba_z) = jnp.split(proj, list(PROJ_OFFSETS), axis=-1)

    def heads(a, h):
        return a.reshape(B, T, h, HEAD_DIM)

    y_nsa = nsa_attention(heads(nsa_q, NSA_HEADS),
                          heads(nsa_kc, NSA_KV_HEADS), heads(nsa_vc, NSA_KV_HEADS),
                          heads(nsa_ks, NSA_KV_HEADS), heads(nsa_vs, NSA_KV_HEADS),
                          heads(nsa_kw, NSA_KV_HEADS), heads(nsa_vw, NSA_KV_HEADS),
                          nsa_g, cmp_pos_k, cmp_w1_k, cmp_b1_k, cmp_w2_k,
                          cmp_pos_v, cmp_w1_v, cmp_b1_v, cmp_w2_v, cos, sin)
    y_moba = moba_attention(heads(moba_q, MOBA_HEADS), heads(moba_k, MOBA_HEADS),
                            heads(moba_v, MOBA_HEADS), cos, sin)
    y = jnp.concatenate([y_nsa * jax.nn.silu(nsa_z), y_moba * jax.nn.silu(moba_z)], axis=-1)
    sub = y @ w_out
    return layer_norm(DEEPNORM_ALPHA * x + sub, ln_gain, ln_bias)


def setup_inputs(seed: int = 0) -> dict:
    key = jax.random.key(seed)
    ks = jax.random.split(key, 13)
    f32 = jnp.float32
    col_scale = np.ones((D_PROJ,), np.float32)
    starts = (0,) + PROJ_OFFSETS
    for slot in VALUE_SLOTS:
        col_scale[starts[slot]:starts[slot] + PROJ_SIZES[slot]] = DEEPNORM_BETA
    fan_c = CMP_BLOCK * HEAD_DIM
    x = jax.random.normal(ks[0], (BATCH, SEQ, D_MODEL), f32)
    w_in = jax.random.normal(ks[1], (DEPTH, D_MODEL, D_PROJ), f32) * (D_MODEL ** -0.5) * jnp.asarray(col_scale)
    cmp_pos_k = 0.02 * jax.random.normal(ks[2], (DEPTH, CMP_BLOCK, HEAD_DIM), f32)
    cmp_w1_k = jax.random.normal(ks[3], (DEPTH, fan_c, CMP_HIDDEN), f32) * fan_c ** -0.5
    cmp_b1_k = 0.01 * jax.random.normal(ks[4], (DEPTH, CMP_HIDDEN), f32)
    cmp_w2_k = jax.random.normal(ks[5], (DEPTH, CMP_HIDDEN, HEAD_DIM), f32) * CMP_HIDDEN ** -0.5
    cmp_pos_v = 0.02 * jax.random.normal(ks[6], (DEPTH, CMP_BLOCK, HEAD_DIM), f32)
    cmp_w1_v = jax.random.normal(ks[7], (DEPTH, fan_c, CMP_HIDDEN), f32) * fan_c ** -0.5
    cmp_b1_v = 0.01 * jax.random.normal(ks[8], (DEPTH, CMP_HIDDEN), f32)
    cmp_w2_v = jax.random.normal(ks[9], (DEPTH, CMP_HIDDEN, HEAD_DIM), f32) * CMP_HIDDEN ** -0.5
    w_out = jax.random.normal(ks[10], (DEPTH, D_MIX, D_MODEL), f32) * (D_MIX ** -0.5) * DEEPNORM_BETA
    ln_gain = 1.0 + 0.02 * jax.random.normal(ks[11], (DEPTH, D_MODEL), f32)
    ln_bias = 0.02 * jax.random.normal(ks[12], (DEPTH, D_MODEL), f32)
    return {"x": x, "w_in": w_in,
            "cmp_pos_k": cmp_pos_k, "cmp_w1_k": cmp_w1_k, "cmp_b1_k": cmp_b1_k, "cmp_w2_k": cmp_w2_k,
            "cmp_pos_v": cmp_pos_v, "cmp_w1_v": cmp_w1_v, "cmp_b1_v": cmp_b1_v, "cmp_w2_v": cmp_w2_v,
            "w_out": w_out, "ln_gain": ln_gain, "ln_bias": ln_bias}


def reference(x, w_in, cmp_pos_k, cmp_w1_k, cmp_b1_k, cmp_w2_k,
              cmp_pos_v, cmp_w1_v, cmp_b1_v, cmp_w2_v, w_out, ln_gain, ln_bias):
    cos, sin = rope_tables(x.shape[1])
    h = x
    for layer in range(DEPTH):
        h = hybrid_layer(h, w_in[layer],
                         cmp_pos_k[layer], cmp_w1_k[layer], cmp_b1_k[layer], cmp_w2_k[layer],
                         cmp_pos_v[layer], cmp_w1_v[layer], cmp_b1_v[layer], cmp_w2_v[layer],
                         w_out[layer], ln_gain[layer], ln_bias[layer], cos, sin)
    return h
```

```python
import functools

import numpy as np
import jax
import jax.numpy as jnp
from jax import lax
from jax.experimental import pallas as pl
from jax.experimental.pallas import tpu as pltpu

D_MODEL = 1024
HEAD_DIM = 64
NSA_HEADS = 8
NSA_KV_HEADS = 2
NSA_GROUP = NSA_HEADS // NSA_KV_HEADS
MOBA_HEADS = 8
D_NSA = NSA_HEADS * HEAD_DIM
D_NSA_KV = NSA_KV_HEADS * HEAD_DIM
D_MOBA = MOBA_HEADS * HEAD_DIM
N_NSA_BRANCHES = 3
ROPE_THETA = 500000.0
ROPE_DIM = HEAD_DIM // 4
CMP_BLOCK = 32
CMP_STRIDE = 16
CMP_HIDDEN = 4 * HEAD_DIM
SEL_BLOCK = 64
SEL_TOPK = 16
WINDOW = 512
MOBA_BLOCK = 256
MOBA_TOPK = 3
LN_EPS = 1e-5
DEPTH = 1
DEEPNORM_ALPHA = (2.0 * DEPTH) ** 0.25
NEG_BIG = -1e30
POS_BIG = 1e30
MASK_BIAS = -1e30
M_INIT = -1e29

LANES = 128
PROJ_TM = 512
ATT_TQ = 256
ATT_TK = 256
VMEM_LIMIT = 52 * 1024 * 1024

F32 = jnp.float32
BF16 = jnp.bfloat16
NT_DIMS = (((1,), (1,)), ((), ()))


def _split2(x):
    hi = x.astype(BF16)
    lo = (x - hi.astype(F32)).astype(BF16)
    return hi, lo


def _split3(x):
    hi = x.astype(BF16)
    r = x - hi.astype(F32)
    mid = r.astype(BF16)
    lo = (r - mid.astype(F32)).astype(BF16)
    return hi, mid, lo


def _dot(a, b):
    return jnp.dot(a, b, preferred_element_type=F32)


def _dot_nt(a, b):
    return lax.dot_general(a, b, NT_DIMS, preferred_element_type=F32)


def _dot3(a_hi, a_lo, b_hi, b_lo, dot=_dot):
    return dot(a_hi, b_hi) + (dot(a_lo, b_hi) + dot(a_hi, b_lo))


def _rope(a, c, sm, sp):
    outs = []
    for p in range(a.shape[1] // LANES):
        s = a[:, p * LANES:(p + 1) * LANES]
        outs.append(s * c + pltpu.roll(s, LANES - ROPE_DIM // 2, 1) * sm + pltpu.roll(s, ROPE_DIM // 2, 1) * sp)
    return outs[0] if len(outs) == 1 else jnp.concatenate(outs, axis=1)


def _sigmoid(x):
    return 1.0 / (1.0 + jnp.exp(-x))


def _block_of(t, block):
    assert block & (block - 1) == 0
    return lax.shift_right_logical(t, jnp.int32(block.bit_length() - 1))


def _proj_precise_kernel(x_ref, w_ref, c_ref, sm_ref, sp_ref,
                         qraw_ref, qrot_ref, kc_ref, mq_ref, mk_ref, km_ref,
                         whi_ref, wlo_ref, *, seq_len):
    i = pl.program_id(0)
    tm = x_ref.shape[0]

    @pl.when(i == 0)
    def _():
        w = w_ref[...]
        hi = w.astype(BF16)
        whi_ref[...] = hi
        wlo_ref[...] = (w - hi.astype(F32)).astype(BF16)

    x_hi, x_lo = _split2(x_ref[...])
    w_hi = whi_ref[...]
    acc = _dot(x_hi, w_hi) + (_dot(x_lo, w_hi) + _dot(x_hi, wlo_ref[...]))
    c, sm, sp = c_ref[...], sm_ref[...], sp_ref[...]

    q = acc[:, 0:D_NSA]
    qraw_ref[...] = q
    qrot_ref[...] = _rope(q, c, sm, sp)
    kc_ref[...] = acc[:, D_NSA:D_NSA + D_NSA_KV]
    o = D_NSA + D_NSA_KV
    mq_ref[...] = _rope(acc[:, o:o + D_MOBA], c, sm, sp)
    mk = _rope(acc[:, o + D_MOBA:o + 2 * D_MOBA], c, sm, sp)

    t = (i * tm) % seq_len + lax.broadcasted_iota(jnp.int32, (tm, LANES), 0)
    lane = lax.broadcasted_iota(jnp.int32, (tm, LANES), 1)
    onehot = jnp.where(lane == _block_of(t, MOBA_BLOCK), 1.0, 0.0).astype(BF16)
    for p in range(D_MOBA // LANES):
        mk_ref[:, 2 * p * LANES:(2 * p + 1) * LANES] = mk[:, p * LANES:(p + 1) * LANES].astype(BF16)
        mk_ref[:, (2 * p + 1) * LANES:(2 * p + 2) * LANES] = onehot
    for r in range(tm // MOBA_BLOCK):
        blk = mk[r * MOBA_BLOCK:(r + 1) * MOBA_BLOCK]
        km_ref[r] = jnp.sum(blk, axis=0, keepdims=True) / MOBA_BLOCK


def _proj_smooth_kernel(x_ref, w_ref, c_ref, sm_ref, sp_ref,
                        vc_ref, ks_ref, vs_ref, kw_ref, vw_ref, g_ref, z_ref, mv_ref, mz_ref,
                        wbf_ref, *, seq_len):
    i = pl.program_id(0)
    tm = x_ref.shape[0]

    @pl.when(i == 0)
    def _():
        wbf_ref[...] = w_ref[...].astype(BF16)

    acc = _dot(x_ref[...].astype(BF16), wbf_ref[...])
    c, sm, sp = c_ref[...], sm_ref[...], sp_ref[...]
    kv = D_NSA_KV
    vc_ref[...] = acc[:, 0:kv]
    t = (i * tm) % seq_len + lax.broadcasted_iota(jnp.int32, (tm, LANES), 0)
    lane = lax.broadcasted_iota(jnp.int32, (tm, LANES), 1)
    ks_ref[:, 0:LANES] = _rope(acc[:, kv:2 * kv], c, sm, sp).astype(BF16)
    ks_ref[:, LANES:2 * LANES] = jnp.where(lane == _block_of(t, SEL_BLOCK), 1.0, 0.0).astype(BF16)
    vs_ref[...] = acc[:, 2 * kv:3 * kv].astype(BF16)
    kw_ref[...] = _rope(acc[:, 3 * kv:4 * kv], c, sm, sp).astype(BF16)
    vw_ref[...] = acc[:, 4 * kv:5 * kv].astype(BF16)
    g_ref[...] = acc[:, 5 * kv:6 * kv]
    o = 6 * kv
    z_ref[...] = acc[:, o:o + D_NSA]
    mv_ref[...] = acc[:, o + D_NSA:o + D_NSA + D_MOBA].astype(BF16)
    mz_ref[...] = acc[:, o + D_NSA + D_MOBA:o + D_NSA + 2 * D_MOBA]


def _row_spec(tm, width):
    return pl.BlockSpec((tm, width), lambda i: (i, 0))


def _table_spec(tm, seq_len):
    nt = seq_len // tm
    return pl.BlockSpec((tm, LANES), lambda i: (i % nt, 0))


def _project(x2, w_prec, w_smooth, tables, seq_len):
    m = x2.shape[0]
    tm = PROJ_TM
    grid = (m // tm,)
    n_prec, n_smooth = w_prec.shape[1], w_smooth.shape[1]
    params = pltpu.CompilerParams(dimension_semantics=("arbitrary",), vmem_limit_bytes=VMEM_LIMIT)
    tab_specs = [_table_spec(tm, seq_len)] * 3

    prec = pl.pallas_call(
        functools.partial(_proj_precise_kernel, seq_len=seq_len),
        grid=grid,
        in_specs=[_row_spec(tm, D_MODEL), pl.BlockSpec((D_MODEL, n_prec), lambda i: (0, 0))] + tab_specs,
        out_specs=[_row_spec(tm, D_NSA), _row_spec(tm, D_NSA), _row_spec(tm, D_NSA_KV), _row_spec(tm, D_MOBA),
                   _row_spec(tm, 2 * D_MOBA),
                   pl.BlockSpec((tm // MOBA_BLOCK, 1, D_MOBA), lambda i: (i, 0, 0))],
        out_shape=[jax.ShapeDtypeStruct((m, D_NSA), F32), jax.ShapeDtypeStruct((m, D_NSA), F32),
                   jax.ShapeDtypeStruct((m, D_NSA_KV), F32), jax.ShapeDtypeStruct((m, D_MOBA), F32),
                   jax.ShapeDtypeStruct((m, 2 * D_MOBA), BF16),
                   jax.ShapeDtypeStruct((m // MOBA_BLOCK, 1, D_MOBA), F32)],
        scratch_shapes=[pltpu.VMEM((D_MODEL, n_prec), BF16), pltpu.VMEM((D_MODEL, n_prec), BF16)],
        compiler_params=params,
        name="proj_precise",
    )(x2, w_prec, *tables)

    smooth = pl.pallas_call(
        functools.partial(_proj_smooth_kernel, seq_len=seq_len),
        grid=grid,
        in_specs=[_row_spec(tm, D_MODEL), pl.BlockSpec((D_MODEL, n_smooth), lambda i: (0, 0))] + tab_specs,
        out_specs=[_row_spec(tm, D_NSA_KV), _row_spec(tm, 2 * LANES), _row_spec(tm, D_NSA_KV),
                   _row_spec(tm, D_NSA_KV), _row_spec(tm, D_NSA_KV), _row_spec(tm, LANES),
                   _row_spec(tm, D_NSA), _row_spec(tm, D_MOBA), _row_spec(tm, D_MOBA)],
        out_shape=[jax.ShapeDtypeStruct((m, D_NSA_KV), F32), jax.ShapeDtypeStruct((m, 2 * LANES), BF16),
                   jax.ShapeDtypeStruct((m, D_NSA_KV), BF16), jax.ShapeDtypeStruct((m, D_NSA_KV), BF16),
                   jax.ShapeDtypeStruct((m, D_NSA_KV), BF16), jax.ShapeDtypeStruct((m, LANES), F32),
                   jax.ShapeDtypeStruct((m, D_NSA), F32), jax.ShapeDtypeStruct((m, D_MOBA), BF16),
                   jax.ShapeDtypeStruct((m, D_MOBA), F32)],
        scratch_shapes=[pltpu.VMEM((D_MODEL, n_smooth), BF16)],
        compiler_params=params,
        name="proj_smooth",
    )(x2, w_smooth, *tables)
    return prec, smooth


def _compress_kernel(ch_ref, pos_ref, w1_ref, b1_ref, w2_ref, out_ref):
    ch = ch_ref[0]
    n = ch.shape[0]
    parts = []
    for half in range(CMP_BLOCK // CMP_STRIDE):
        a_hi, a_lo = _split2(ch + pos_ref[half])
        w_hi, w_lo = _split2(w1_ref[half])
        parts.append(_dot3(a_hi, a_lo, w_hi, w_lo))
    nxt = pltpu.roll(parts[1], n - 1, 0)
    row = lax.broadcasted_iota(jnp.int32, nxt.shape, 0)
    h = parts[0] + jnp.where(row < n - 1, nxt, 0.0) + b1_ref[...]
    h = h * _sigmoid(h)
    h_hi, h_lo = _split2(h)
    w_hi, w_lo = _split2(w2_ref[...])
    out_ref[0] = _dot3(h_hi, h_lo, w_hi, w_lo)


def _compress(raw, pos, w1, b1, w2, batch, seq_len):
    g = NSA_KV_HEADS
    n_chunks = seq_len // CMP_STRIDE
    feat = CMP_STRIDE * D_NSA_KV
    ch = raw.reshape(batch, n_chunks, feat)
    n_half = CMP_BLOCK // CMP_STRIDE
    eye = jnp.eye(g, dtype=F32)
    w1r = w1.reshape(n_half, CMP_STRIDE, HEAD_DIM, CMP_HIDDEN)
    w1b = jnp.einsum('aldh,gk->algdkh', w1r, eye).reshape(n_half, feat, g * CMP_HIDDEN)
    posb = jnp.broadcast_to(pos.reshape(n_half, CMP_STRIDE, 1, HEAD_DIM),
                            (n_half, CMP_STRIDE, g, HEAD_DIM)).reshape(n_half, 1, feat)
    b1b = jnp.tile(b1.reshape(1, CMP_HIDDEN), (1, g))
    w2b = jnp.einsum('hd,gk->ghkd', w2, eye).reshape(g * CMP_HIDDEN, g * HEAD_DIM)
    return pl.pallas_call(
        _compress_kernel,
        grid=(batch,),
        in_specs=[pl.BlockSpec((1, n_chunks, feat), lambda b: (b, 0, 0)),
                  pl.BlockSpec((n_half, 1, feat), lambda b: (0, 0, 0)),
                  pl.BlockSpec((n_half, feat, g * CMP_HIDDEN), lambda b: (0, 0, 0)),
                  pl.BlockSpec((1, g * CMP_HIDDEN), lambda b: (0, 0)),
                  pl.BlockSpec((g * CMP_HIDDEN, g * HEAD_DIM), lambda b: (0, 0))],
        out_specs=pl.BlockSpec((1, n_chunks, g * HEAD_DIM), lambda b: (b, 0, 0)),
        out_shape=jax.ShapeDtypeStruct((batch, n_chunks, g * HEAD_DIM), F32),
        compiler_params=pltpu.CompilerParams(dimension_semantics=("arbitrary",), vmem_limit_bytes=VMEM_LIMIT),
        name="compress",
    )(ch, posb, w1b, b1b, w2b)


def _top_k_mask(val, k):
    lane = lax.broadcasted_iota(jnp.int32, val.shape, 1).astype(F32)
    sel = jnp.zeros(val.shape, F32)
    for _ in range(k):
        m = jnp.max(val, axis=1, keepdims=True)
        idx = jnp.min(jnp.where(val == m, lane, float(LANES)), axis=1, keepdims=True)
        hit = lane == idx
        sel = jnp.where(hit, 1.0, sel)
        val = jnp.where(hit, -jnp.inf, val)
    return sel > 0.5


def _masked_softmax(s, mask):
    s = jnp.where(mask, s, NEG_BIG)
    m = jnp.max(s, axis=1, keepdims=True)
    e = jnp.where(mask, jnp.exp(s - m), 0.0)
    d = jnp.sum(e, axis=1, keepdims=True)
    return e / jnp.where(d > 0, d, 1.0)


def _online_step(s, v, m_ref, l_ref, acc_ref):
    m_prev = m_ref[...]
    m_new = jnp.maximum(m_prev, jnp.max(s, axis=1, keepdims=True))
    alpha = jnp.exp(m_prev - m_new)
    p = jnp.exp(s - m_new)
    l_ref[...] = alpha * l_ref[...] + jnp.sum(p, axis=1, keepdims=True)
    acc_ref[...] = alpha * acc_ref[...] + _dot(p.astype(BF16), v)
    m_ref[...] = m_new


def _half_select(blk, src_half, dst_half):
    if src_half != dst_half:
        blk = pltpu.roll(blk, HEAD_DIM, 1)
    lane = lax.broadcasted_iota(jnp.int32, blk.shape, 1)
    return jnp.where((lane >= dst_half * HEAD_DIM) & (lane < (dst_half + 1) * HEAD_DIM), blk, 0.0)


def _nsa_kernel(qraw_ref, qrot_ref, kc_ref, vc_ref, ks_ref, vs_ref, kw_ref, vw_ref, g_ref, z_ref,
                ov_ref, eg_ref, y_ref, m_ref, l_ref, acc_ref):
    i = pl.program_id(1)
    tq, tk = ATT_TQ, ATT_TK
    r = NSA_GROUP
    s0 = i * tq
    scale = HEAD_DIM ** -0.5
    t = s0 + lax.broadcasted_iota(jnp.int32, (tq, 1), 0)
    t_all = jnp.concatenate([t] * r, axis=0)
    lane = lax.broadcasted_iota(jnp.int32, (tq, LANES), 1)

    kc = kc_ref[0]
    n_cmp = kc.shape[0]
    kc_hi, kc_lo = _split2(kc)
    vc_bf = vc_ref[0].astype(BF16)
    cmp_end = lax.broadcasted_iota(jnp.int32, (1, n_cmp), 1) * CMP_STRIDE + (CMP_BLOCK - 1)
    cmp_mask = cmp_end <= t_all

    win_start = jnp.maximum(s0 - WINDOW, 0)
    n_win = WINDOW + tq
    kwin = kw_ref[0, pl.ds(pl.multiple_of(win_start, tq), n_win), :]
    vwin = vw_ref[0, pl.ds(pl.multiple_of(win_start, tq), n_win), :]
    kpos_w = win_start + lax.broadcasted_iota(jnp.int32, (1, n_win), 1)
    win_mask = (kpos_w <= t_all) & (kpos_w > t_all - WINDOW)

    qraw = qraw_ref[0]
    qrot = qrot_ref[0]
    o_cmp, o_slc, o_win = [], [], []
    for g in range(NSA_KV_HEADS):
        heads = [g * r + j for j in range(r)]
        qn = jnp.concatenate(
            [_half_select(qraw[:, (h // 2) * LANES:(h // 2 + 1) * LANES], h % 2, g) for h in heads], axis=0) * scale
        qr = jnp.concatenate(
            [_half_select(qrot[:, (h // 2) * LANES:(h // 2 + 1) * LANES], h % 2, g) for h in heads], axis=0) * scale
        qr_bf = qr.astype(BF16)

        qn_hi, qn_lo = _split2(qn)
        pc = _masked_softmax(_dot3(qn_hi, qn_lo, kc_hi, kc_lo, dot=_dot_nt), cmp_mask)
        o_cmp.append(_dot(pc.astype(BF16), vc_bf))

        pc_sum = pc[0:tq]
        for j in range(1, r):
            pc_sum = pc_sum + pc[j * tq:(j + 1) * tq]
        p_hi, p_mid, p_lo = _split3(pc_sum)
        ov = ov_ref[...]
        imp = _dot(p_hi, ov) + (_dot(p_mid, ov) + _dot(p_lo, ov))
        qblk = _block_of(t, SEL_BLOCK)
        forced = (lane == 0) | (lane == qblk) | (lane == qblk - 1)
        imp = jnp.where(forced, POS_BIG, jnp.where(lane > qblk, NEG_BIG, imp))
        sel = _top_k_mask(imp, SEL_TOPK)
        sel_bias = jnp.where(sel, 0.0, MASK_BIAS).astype(BF16)
        qs = jnp.concatenate([qr_bf, jnp.concatenate([sel_bias] * r, axis=0)], axis=1)

        m_ref[...] = jnp.full(m_ref.shape, M_INIT, F32)
        l_ref[...] = jnp.zeros(l_ref.shape, F32)
        acc_ref[...] = jnp.zeros(acc_ref.shape, F32)

        def body(j, carry):
            start = pl.multiple_of(j * tk, tk)
            s = _dot_nt(qs, ks_ref[0, pl.ds(start, tk), :])
            _online_step(s, vs_ref[0, pl.ds(start, tk), :], m_ref, l_ref, acc_ref)
            return carry

        lax.fori_loop(0, i, body, 0)
        start = pl.multiple_of(s0, tk)
        s = _dot_nt(qs, ks_ref[0, pl.ds(start, tk), :])
        kpos = s0 + lax.broadcasted_iota(jnp.int32, (1, tk), 1)
        s = jnp.where(kpos <= t_all, s, MASK_BIAS)
        _online_step(s, vs_ref[0, pl.ds(start, tk), :], m_ref, l_ref, acc_ref)
        o_slc.append(acc_ref[...] / l_ref[...])

        pw = _masked_softmax(_dot_nt(qr_bf, kwin), win_mask)
        o_win.append(_dot(pw.astype(BF16), vwin))

    g_hi, g_lo = _split2(_sigmoid(g_ref[0]))
    eg = eg_ref[...]
    gates = _dot(g_hi, eg) + _dot(g_lo, eg)
    z = z_ref[0]
    for p in range(D_NSA // LANES):
        g = (2 * p) // r
        y = None
        for b, o_all in enumerate((o_cmp, o_slc, o_win)):
            o = o_all[g]
            j0, j1 = (2 * p) % r, (2 * p + 1) % r
            a0 = o[j0 * tq:(j0 + 1) * tq]
            a1 = o[j1 * tq:(j1 + 1) * tq]
            if g != 0:
                a0 = pltpu.roll(a0, HEAD_DIM, 1)
            if g != 1:
                a1 = pltpu.roll(a1, HEAD_DIM, 1)
            pair = jnp.where(lane < HEAD_DIM, a0, a1)
            term = gates[:, b * D_NSA + p * LANES:b * D_NSA + (p + 1) * LANES] * pair
            y = term if y is None else y + term
        zp = z[:, p * LANES:(p + 1) * LANES]
        y_ref[0, :, p * LANES:(p + 1) * LANES] = (y * (zp * _sigmoid(zp))).astype(BF16)


def _nsa_attention(qraw, qrot, kc, vc, ks_ext, vs, kw, vw, gl, z, batch, seq_len):
    tq = ATT_TQ
    n_cmp = seq_len // CMP_STRIDE
    n_sel = seq_len // SEL_BLOCK
    c_start = np.arange(n_cmp)[:, None] * CMP_STRIDE
    s_start = np.arange(LANES)[None, :] * SEL_BLOCK
    overlap = ((c_start < s_start + SEL_BLOCK) & (c_start + CMP_BLOCK > s_start) & (np.arange(LANES)[None, :] < n_sel))
    overlap = jnp.asarray(overlap.astype(np.float32), dtype=BF16)
    eg = np.zeros((LANES, N_NSA_BRANCHES * D_NSA), np.float32)
    for h in range(NSA_HEADS):
        for b in range(N_NSA_BRANCHES):
            eg[h * N_NSA_BRANCHES + b, b * D_NSA + h * HEAD_DIM:b * D_NSA + (h + 1) * HEAD_DIM] = 1.0
    eg = jnp.asarray(eg, dtype=BF16)

    def r3(a):
        return a.reshape(batch, seq_len, a.shape[-1])

    tile = lambda w: pl.BlockSpec((1, tq, w), lambda b, i: (b, i, 0))
    full = lambda n, w: pl.BlockSpec((1, n, w), lambda b, i: (b, 0, 0))
    const = lambda shape: pl.BlockSpec(shape, lambda b, i: (0, 0))
    rows = NSA_GROUP * tq
    return pl.pallas_call(
        _nsa_kernel,
        grid=(batch, seq_len // tq),
        in_specs=[tile(D_NSA), tile(D_NSA), full(n_cmp, D_NSA_KV), full(n_cmp, D_NSA_KV),
                  full(seq_len, 2 * LANES), full(seq_len, D_NSA_KV), full(seq_len, D_NSA_KV),
                  full(seq_len, D_NSA_KV), tile(LANES), tile(D_NSA),
                  const(overlap.shape), const(eg.shape)],
        out_specs=tile(D_NSA),
        out_shape=jax.ShapeDtypeStruct((batch, seq_len, D_NSA), BF16),
        scratch_shapes=[pltpu.VMEM((rows, 1), F32), pltpu.VMEM((rows, 1), F32), pltpu.VMEM((rows, LANES), F32)],
        compiler_params=pltpu.CompilerParams(dimension_semantics=("arbitrary", "arbitrary"),
                                             vmem_limit_bytes=VMEM_LIMIT),
        name="nsa_attention",
    )(r3(qraw), r3(qrot), kc, vc, r3(ks_ext), r3(vs), r3(kw), r3(vw), r3(gl), r3(z), overlap, eg)


def _moba_kernel(q_ref, km_ref, mk_ref, mv_ref, z_ref, y_ref, m_ref, l_ref, acc_ref):
    own = pl.program_id(2)
    tq = ATT_TQ
    s0 = own * tq
    scale = HEAD_DIM ** -0.5
    t = s0 + lax.broadcasted_iota(jnp.int32, (tq, 1), 0)
    t_all = jnp.concatenate([t, t], axis=0)
    lane = lax.broadcasted_iota(jnp.int32, (tq, LANES), 1)

    q = q_ref[0]
    km = km_ref[0]
    nb = km.shape[0]
    if nb < LANES:
        km = jnp.concatenate([km, jnp.zeros((LANES - nb, LANES), F32)], axis=0)
    km_hi, km_lo = _split2(km)

    qs = []
    for hh in range(2):
        qh = _half_select(q, hh, hh)
        q_hi, q_lo = _split2(qh)
        sg = _dot3(q_hi, q_lo, km_hi, km_lo, dot=_dot_nt)
        sg = jnp.where(lane < own, sg, NEG_BIG)
        valid = _top_k_mask(sg, MOBA_TOPK) & (lane < own)
        bias = jnp.where(valid, 0.0, MASK_BIAS).astype(BF16)
        qs.append(jnp.concatenate([(qh * scale).astype(BF16), bias], axis=1))
    qs = jnp.concatenate(qs, axis=0)

    m_ref[...] = jnp.full(m_ref.shape, M_INIT, F32)
    l_ref[...] = jnp.zeros(l_ref.shape, F32)
    acc_ref[...] = jnp.zeros(acc_ref.shape, F32)

    def body(j, carry):
        start = pl.multiple_of(j * tq, tq)
        s = _dot_nt(qs, mk_ref[0, pl.ds(start, tq), :])
        _online_step(s, mv_ref[0, pl.ds(start, tq), :], m_ref, l_ref, acc_ref)
        return carry

    lax.fori_loop(0, own, body, 0)
    start = pl.multiple_of(s0, tq)
    kd = mk_ref[0, pl.ds(start, tq), :]
    s = _dot_nt(qs[:, 0:LANES], kd[:, 0:LANES])
    kpos = s0 + lax.broadcasted_iota(jnp.int32, (1, tq), 1)
    s = jnp.where(kpos <= t_all, s, MASK_BIAS)
    _online_step(s, mv_ref[0, pl.ds(start, tq), :], m_ref, l_ref, acc_ref)
    o = acc_ref[...] / l_ref[...]
    y = jnp.where(lane < HEAD_DIM, o[0:tq], o[tq:2 * tq])
    z = z_ref[0]
    y_ref[0] = (y * (z * _sigmoid(z))).astype(BF16)


def _moba_attention(mq, kmean, mk_ext, mv, mz, batch, seq_len):
    tq = ATT_TQ
    nb = seq_len // MOBA_BLOCK
    n_pairs = D_MOBA // LANES

    def r3(a):
        return a.reshape(batch, seq_len, a.shape[-1])

    return pl.pallas_call(
        _moba_kernel,
        grid=(batch, n_pairs, seq_len // tq),
        in_specs=[pl.BlockSpec((1, tq, LANES), lambda b, p, i: (b, i, p)),
                  pl.BlockSpec((1, nb, LANES), lambda b, p, i: (b, 0, p)),
                  pl.BlockSpec((1, seq_len, 2 * LANES), lambda b, p, i: (b, 0, p)),
                  pl.BlockSpec((1, seq_len, LANES), lambda b, p, i: (b, 0, p)),
                  pl.BlockSpec((1, tq, LANES), lambda b, p, i: (b, i, p))],
        out_specs=pl.BlockSpec((1, tq, LANES), lambda b, p, i: (b, i, p)),
        out_shape=jax.ShapeDtypeStruct((batch, seq_len, D_MOBA), BF16),
        scratch_shapes=[pltpu.VMEM((2 * tq, 1), F32), pltpu.VMEM((2 * tq, 1), F32),
                        pltpu.VMEM((2 * tq, LANES), F32)],
        compiler_params=pltpu.CompilerParams(dimension_semantics=("arbitrary", "arbitrary", "arbitrary"),
                                             vmem_limit_bytes=VMEM_LIMIT),
        name="moba_attention",
    )(r3(mq), kmean.reshape(batch, nb, D_MOBA), r3(mk_ext), r3(mv), r3(mz))


def _out_kernel(x_ref, yn_ref, ym_ref, w_ref, gain_ref, bias_ref, o_ref, wbf_ref):
    @pl.when(pl.program_id(0) == 0)
    def _():
        wbf_ref[...] = w_ref[...].astype(BF16)

    sub = _dot(yn_ref[...], wbf_ref[0:D_NSA, :]) + _dot(ym_ref[...], wbf_ref[D_NSA:D_NSA + D_MOBA, :])
    h = DEEPNORM_ALPHA * x_ref[...] + sub
    mu = jnp.mean(h, axis=1, keepdims=True)
    d = h - mu
    var = jnp.mean(d * d, axis=1, keepdims=True)
    o_ref[...] = d * lax.rsqrt(var + LN_EPS) * gain_ref[...] + bias_ref[...]


def _out_project(x2, y_nsa, y_moba, w_out, gain, bias):
    m = x2.shape[0]
    tm = PROJ_TM
    d_mix = D_NSA + D_MOBA
    return pl.pallas_call(
        _out_kernel,
        grid=(m // tm,),
        in_specs=[_row_spec(tm, D_MODEL), _row_spec(tm, D_NSA), _row_spec(tm, D_MOBA),
                  pl.BlockSpec((d_mix, D_MODEL), lambda i: (0, 0)),
                  pl.BlockSpec((1, D_MODEL), lambda i: (0, 0)), pl.BlockSpec((1, D_MODEL), lambda i: (0, 0))],
        out_specs=_row_spec(tm, D_MODEL),
        out_shape=jax.ShapeDtypeStruct((m, D_MODEL), F32),
        scratch_shapes=[pltpu.VMEM((d_mix, D_MODEL), BF16)],
        compiler_params=pltpu.CompilerParams(dimension_semantics=("arbitrary",), vmem_limit_bytes=VMEM_LIMIT),
        name="out_proj_norm",
    )(x2, y_nsa, y_moba, w_out, gain.reshape(1, D_MODEL), bias.reshape(1, D_MODEL))


def _rope_tables(seq_len):
    half = ROPE_DIM // 2
    inv_freq = ROPE_THETA ** (-jnp.arange(0, ROPE_DIM, 2, dtype=F32) / ROPE_DIM)
    ang = jnp.arange(seq_len, dtype=F32)[:, None] * inv_freq[None, :]
    cos, sin = jnp.cos(ang), jnp.sin(ang)
    zeros = jnp.zeros((seq_len, HEAD_DIM - ROPE_DIM), F32)
    zh = jnp.zeros((seq_len, half), F32)
    c = jnp.concatenate([cos, cos, zeros + 1.0], axis=1)
    sm = jnp.concatenate([-sin, zh, zeros], axis=1)
    sp = jnp.concatenate([zh, sin, zeros], axis=1)
    return tuple(jnp.tile(a, (1, LANES // HEAD_DIM)) for a in (c, sm, sp))


def _layer(x, w_in, cmp_pos_k, cmp_w1_k, cmp_b1_k, cmp_w2_k, cmp_pos_v, cmp_w1_v, cmp_b1_v, cmp_w2_v,
           w_out, ln_gain, ln_bias, tables):
    batch, seq_len, _ = x.shape
    kv = D_NSA_KV
    sizes = (D_NSA, kv, kv, kv, kv, kv, kv, N_NSA_BRANCHES * NSA_HEADS, D_NSA, D_MOBA, D_MOBA, D_MOBA, D_MOBA)
    offs = np.concatenate([[0], np.cumsum(sizes)])
    col = lambda s: w_in[:, int(offs[s]):int(offs[s + 1])]
    w_prec = jnp.concatenate([col(0), col(1), col(9), col(10)], axis=1)
    gate_pad = jnp.zeros((D_MODEL, LANES - sizes[7]), F32)
    w_smooth = jnp.concatenate([col(2), col(3), col(4), col(5), col(6), col(7), gate_pad,
                                col(8), col(11), col(12)], axis=1)

    x2 = x.reshape(batch * seq_len, D_MODEL)
    (qraw, qrot, kc_raw, mq, mk_ext, kmean), (vc_raw, ks_ext, vs, kw, vw, gl, z, mv, mz) = _project(
        x2, w_prec, w_smooth, tables, seq_len)
    kc = _compress(kc_raw, cmp_pos_k, cmp_w1_k, cmp_b1_k, cmp_w2_k, batch, seq_len)
    vc = _compress(vc_raw, cmp_pos_v, cmp_w1_v, cmp_b1_v, cmp_w2_v, batch, seq_len)
    y_nsa = _nsa_attention(qraw, qrot, kc, vc, ks_ext, vs, kw, vw, gl, z, batch, seq_len)
    y_moba = _moba_attention(mq, kmean, mk_ext, mv, mz, batch, seq_len)
    out = _out_project(x2, y_nsa.reshape(batch * seq_len, D_NSA), y_moba.reshape(batch * seq_len, D_MOBA),
                       w_out, ln_gain, ln_bias)
    return out.reshape(batch, seq_len, D_MODEL)


def kernel(x, w_in, cmp_pos_k, cmp_w1_k, cmp_b1_k, cmp_w2_k, cmp_pos_v, cmp_w1_v, cmp_b1_v, cmp_w2_v,
           w_out, ln_gain, ln_bias):
    tables = _rope_tables(x.shape[1])
    h = x
    for layer in range(w_in.shape[0]):
        h = _layer(h, w_in[layer], cmp_pos_k[layer], cmp_w1_k[layer], cmp_b1_k[layer], cmp_w2_k[layer],
                   cmp_pos_v[layer], cmp_w1_v[layer], cmp_b1_v[layer], cmp_w2_v[layer],
                   w_out[layer], ln_gain[layer], ln_bias[layer], tables)
    return h
```

```python
import functools

import numpy as np
import jax
import jax.numpy as jnp
from jax import lax
from jax.experimental import pallas as pl
from jax.experimental.pallas import tpu as pltpu

D_MODEL = 1024
HEAD_DIM = 64
NSA_HEADS = 8
NSA_KV_HEADS = 2
NSA_GROUP = NSA_HEADS // NSA_KV_HEADS
MOBA_HEADS = 8
D_NSA = NSA_HEADS * HEAD_DIM
D_NSA_KV = NSA_KV_HEADS * HEAD_DIM
D_MOBA = MOBA_HEADS * HEAD_DIM
N_NSA_BRANCHES = 3
ROPE_THETA = 500000.0
ROPE_DIM = HEAD_DIM // 4
CMP_BLOCK = 32
CMP_STRIDE = 16
CMP_HIDDEN = 4 * HEAD_DIM
SEL_BLOCK = 64
SEL_TOPK = 16
WINDOW = 512
MOBA_BLOCK = 256
MOBA_TOPK = 3
LN_EPS = 1e-5
DEPTH = 1
DEEPNORM_ALPHA = (2.0 * DEPTH) ** 0.25
NEG_BIG = -1e30
POS_BIG = 1e30
MASK_BIAS = -1e30
M_INIT = -1e29

LANES = 128
PROJ_TM = 512
ATT_TQ = 256
ATT_TK = 512
VMEM_LIMIT = 52 * 1024 * 1024

F32 = jnp.float32
BF16 = jnp.bfloat16
NT_DIMS = (((1,), (1,)), ((), ()))


def _split2(x):
    hi = x.astype(BF16)
    lo = (x - hi.astype(F32)).astype(BF16)
    return hi, lo


def _split3(x):
    hi = x.astype(BF16)
    r = x - hi.astype(F32)
    mid = r.astype(BF16)
    lo = (r - mid.astype(F32)).astype(BF16)
    return hi, mid, lo


def _dot(a, b):
    return jnp.dot(a, b, preferred_element_type=F32)


def _dot_nt(a, b):
    return lax.dot_general(a, b, NT_DIMS, preferred_element_type=F32)


def _dot3(a_hi, a_lo, b_hi, b_lo, dot=_dot):
    return dot(a_hi, b_hi) + (dot(a_lo, b_hi) + dot(a_hi, b_lo))


def _rope(a, c, sm, sp):
    outs = []
    for p in range(a.shape[1] // LANES):
        s = a[:, p * LANES:(p + 1) * LANES]
        outs.append(s * c + pltpu.roll(s, LANES - ROPE_DIM // 2, 1) * sm + pltpu.roll(s, ROPE_DIM // 2, 1) * sp)
    return outs[0] if len(outs) == 1 else jnp.concatenate(outs, axis=1)


def _sigmoid(x):
    return 1.0 / (1.0 + jnp.exp(-x))


def _block_of(t, block):
    assert block & (block - 1) == 0
    return lax.shift_right_logical(t, jnp.int32(block.bit_length() - 1))


def _proj_precise_kernel(x_ref, w_ref, c_ref, sm_ref, sp_ref,
                         qraw_ref, qrot_ref, kc_ref, mq_ref, mk_ref, km_ref,
                         whi_ref, wlo_ref, *, seq_len):
    i = pl.program_id(0)
    tm = x_ref.shape[0]

    @pl.when(i == 0)
    def _():
        w = w_ref[...]
        hi = w.astype(BF16)
        whi_ref[...] = hi
        wlo_ref[...] = (w - hi.astype(F32)).astype(BF16)

    x_hi, x_lo = _split2(x_ref[...])
    w_hi = whi_ref[...]
    acc = _dot(x_hi, w_hi) + (_dot(x_lo, w_hi) + _dot(x_hi, wlo_ref[...]))
    c, sm, sp = c_ref[...], sm_ref[...], sp_ref[...]

    q = acc[:, 0:D_NSA]
    qraw_ref[...] = q
    qrot_ref[...] = _rope(q, c, sm, sp)
    kc_ref[...] = acc[:, D_NSA:D_NSA + D_NSA_KV]
    o = D_NSA + D_NSA_KV
    mq_ref[...] = _rope(acc[:, o:o + D_MOBA], c, sm, sp)
    mk = _rope(acc[:, o + D_MOBA:o + 2 * D_MOBA], c, sm, sp)

    t = (i * tm) % seq_len + lax.broadcasted_iota(jnp.int32, (tm, LANES), 0)
    lane = lax.broadcasted_iota(jnp.int32, (tm, LANES), 1)
    onehot = jnp.where(lane == _block_of(t, MOBA_BLOCK), 1.0, 0.0).astype(BF16)
    for p in range(D_MOBA // LANES):
        mk_ref[:, 2 * p * LANES:(2 * p + 1) * LANES] = mk[:, p * LANES:(p + 1) * LANES].astype(BF16)
        mk_ref[:, (2 * p + 1) * LANES:(2 * p + 2) * LANES] = onehot
    for r in range(tm // MOBA_BLOCK):
        blk = mk[r * MOBA_BLOCK:(r + 1) * MOBA_BLOCK]
        km_ref[r] = jnp.sum(blk, axis=0, keepdims=True) / MOBA_BLOCK


def _proj_smooth_kernel(x_ref, w_ref, c_ref, sm_ref, sp_ref,
                        vc_ref, ks_ref, vs_ref, kw_ref, vw_ref, g_ref, z_ref, mv_ref, mz_ref,
                        wbf_ref, *, seq_len):
    i = pl.program_id(0)
    tm = x_ref.shape[0]

    @pl.when(i == 0)
    def _():
        wbf_ref[...] = w_ref[...].astype(BF16)

    acc = _dot(x_ref[...].astype(BF16), wbf_ref[...])
    c, sm, sp = c_ref[...], sm_ref[...], sp_ref[...]
    kv = D_NSA_KV
    vc_ref[...] = acc[:, 0:kv]
    t = (i * tm) % seq_len + lax.broadcasted_iota(jnp.int32, (tm, LANES), 0)
    lane = lax.broadcasted_iota(jnp.int32, (tm, LANES), 1)
    ks_ref[:, 0:LANES] = _rope(acc[:, kv:2 * kv], c, sm, sp).astype(BF16)
    ks_ref[:, LANES:2 * LANES] = jnp.where(lane == _block_of(t, SEL_BLOCK), 1.0, 0.0).astype(BF16)
    ones = jnp.ones((tm, LANES), BF16)
    vs_ref[:, 0:LANES] = acc[:, 2 * kv:3 * kv].astype(BF16)
    vs_ref[:, LANES:2 * LANES] = ones
    kw_ref[...] = _rope(acc[:, 3 * kv:4 * kv], c, sm, sp).astype(BF16)
    vw_ref[...] = acc[:, 4 * kv:5 * kv].astype(BF16)
    g_ref[...] = acc[:, 5 * kv:6 * kv]
    o = 6 * kv
    z_ref[...] = acc[:, o:o + D_NSA]
    mv = acc[:, o + D_NSA:o + D_NSA + D_MOBA].astype(BF16)
    for p in range(D_MOBA // LANES):
        mv_ref[:, 2 * p * LANES:(2 * p + 1) * LANES] = mv[:, p * LANES:(p + 1) * LANES]
        mv_ref[:, (2 * p + 1) * LANES:(2 * p + 2) * LANES] = ones
    mz_ref[...] = acc[:, o + D_NSA + D_MOBA:o + D_NSA + 2 * D_MOBA]


def _row_spec(tm, width):
    return pl.BlockSpec((tm, width), lambda i: (i, 0))


def _table_spec(tm, seq_len):
    nt = seq_len // tm
    return pl.BlockSpec((tm, LANES), lambda i: (i % nt, 0))


def _project(x2, w_prec, w_smooth, tables, seq_len):
    m = x2.shape[0]
    tm = PROJ_TM
    grid = (m // tm,)
    n_prec, n_smooth = w_prec.shape[1], w_smooth.shape[1]
    params = pltpu.CompilerParams(dimension_semantics=("arbitrary",), vmem_limit_bytes=VMEM_LIMIT)
    tab_specs = [_table_spec(tm, seq_len)] * 3

    prec = pl.pallas_call(
        functools.partial(_proj_precise_kernel, seq_len=seq_len),
        grid=grid,
        in_specs=[_row_spec(tm, D_MODEL), pl.BlockSpec((D_MODEL, n_prec), lambda i: (0, 0))] + tab_specs,
        out_specs=[_row_spec(tm, D_NSA), _row_spec(tm, D_NSA), _row_spec(tm, D_NSA_KV), _row_spec(tm, D_MOBA),
                   _row_spec(tm, 2 * D_MOBA),
                   pl.BlockSpec((tm // MOBA_BLOCK, 1, D_MOBA), lambda i: (i, 0, 0))],
        out_shape=[jax.ShapeDtypeStruct((m, D_NSA), F32), jax.ShapeDtypeStruct((m, D_NSA), F32),
                   jax.ShapeDtypeStruct((m, D_NSA_KV), F32), jax.ShapeDtypeStruct((m, D_MOBA), F32),
                   jax.ShapeDtypeStruct((m, 2 * D_MOBA), BF16),
                   jax.ShapeDtypeStruct((m // MOBA_BLOCK, 1, D_MOBA), F32)],
        scratch_shapes=[pltpu.VMEM((D_MODEL, n_prec), BF16), pltpu.VMEM((D_MODEL, n_prec), BF16)],
        compiler_params=params,
        name="proj_precise",
    )(x2, w_prec, *tables)

    smooth = pl.pallas_call(
        functools.partial(_proj_smooth_kernel, seq_len=seq_len),
        grid=grid,
        in_specs=[_row_spec(tm, D_MODEL), pl.BlockSpec((D_MODEL, n_smooth), lambda i: (0, 0))] + tab_specs,
        out_specs=[_row_spec(tm, D_NSA_KV), _row_spec(tm, 2 * LANES), _row_spec(tm, 2 * LANES),
                   _row_spec(tm, D_NSA_KV), _row_spec(tm, D_NSA_KV), _row_spec(tm, LANES),
                   _row_spec(tm, D_NSA), _row_spec(tm, 2 * D_MOBA), _row_spec(tm, D_MOBA)],
        out_shape=[jax.ShapeDtypeStruct((m, D_NSA_KV), F32), jax.ShapeDtypeStruct((m, 2 * LANES), BF16),
                   jax.ShapeDtypeStruct((m, 2 * LANES), BF16), jax.ShapeDtypeStruct((m, D_NSA_KV), BF16),
                   jax.ShapeDtypeStruct((m, D_NSA_KV), BF16), jax.ShapeDtypeStruct((m, LANES), F32),
                   jax.ShapeDtypeStruct((m, D_NSA), F32), jax.ShapeDtypeStruct((m, 2 * D_MOBA), BF16),
                   jax.ShapeDtypeStruct((m, D_MOBA), F32)],
        scratch_shapes=[pltpu.VMEM((D_MODEL, n_smooth), BF16)],
        compiler_params=params,
        name="proj_smooth",
    )(x2, w_smooth, *tables)
    return prec, smooth


def _compress_kernel(ch_ref, pos_ref, w1_ref, b1_ref, w2_ref, out_ref):
    ch = ch_ref[0]
    n = ch.shape[0]
    parts = []
    for half in range(CMP_BLOCK // CMP_STRIDE):
        a_hi, a_lo = _split2(ch + pos_ref[half])
        w_hi, w_lo = _split2(w1_ref[half])
        parts.append(_dot3(a_hi, a_lo, w_hi, w_lo))
    nxt = pltpu.roll(parts[1], n - 1, 0)
    row = lax.broadcasted_iota(jnp.int32, nxt.shape, 0)
    h = parts[0] + jnp.where(row < n - 1, nxt, 0.0) + b1_ref[...]
    h = h * _sigmoid(h)
    h_hi, h_lo = _split2(h)
    w_hi, w_lo = _split2(w2_ref[...])
    out_ref[0] = _dot3(h_hi, h_lo, w_hi, w_lo)


def _compress(raw, pos, w1, b1, w2, batch, seq_len):
    g = NSA_KV_HEADS
    n_chunks = seq_len // CMP_STRIDE
    feat = CMP_STRIDE * D_NSA_KV
    ch = raw.reshape(batch, n_chunks, feat)
    n_half = CMP_BLOCK // CMP_STRIDE
    eye = jnp.eye(g, dtype=F32)
    w1r = w1.reshape(n_half, CMP_STRIDE, HEAD_DIM, CMP_HIDDEN)
    w1b = jnp.einsum('aldh,gk->algdkh', w1r, eye).reshape(n_half, feat, g * CMP_HIDDEN)
    posb = jnp.broadcast_to(pos.reshape(n_half, CMP_STRIDE, 1, HEAD_DIM),
                            (n_half, CMP_STRIDE, g, HEAD_DIM)).reshape(n_half, 1, feat)
    b1b = jnp.tile(b1.reshape(1, CMP_HIDDEN), (1, g))
    w2b = jnp.einsum('hd,gk->ghkd', w2, eye).reshape(g * CMP_HIDDEN, g * HEAD_DIM)
    return pl.pallas_call(
        _compress_kernel,
        grid=(batch,),
        in_specs=[pl.BlockSpec((1, n_chunks, feat), lambda b: (b, 0, 0)),
                  pl.BlockSpec((n_half, 1, feat), lambda b: (0, 0, 0)),
                  pl.BlockSpec((n_half, feat, g * CMP_HIDDEN), lambda b: (0, 0, 0)),
                  pl.BlockSpec((1, g * CMP_HIDDEN), lambda b: (0, 0)),
                  pl.BlockSpec((g * CMP_HIDDEN, g * HEAD_DIM), lambda b: (0, 0))],
        out_specs=pl.BlockSpec((1, n_chunks, g * HEAD_DIM), lambda b: (b, 0, 0)),
        out_shape=jax.ShapeDtypeStruct((batch, n_chunks, g * HEAD_DIM), F32),
        compiler_params=pltpu.CompilerParams(dimension_semantics=("arbitrary",), vmem_limit_bytes=VMEM_LIMIT),
        name="compress",
    )(ch, posb, w1b, b1b, w2b)


def _top_k_rows(val, k):
    row = lax.broadcasted_iota(jnp.int32, val.shape, 0).astype(F32)
    sel = jnp.zeros(val.shape, F32)
    for _ in range(k):
        m = jnp.max(val, axis=0, keepdims=True)
        idx = jnp.min(jnp.where(val == m, row, float(LANES)), axis=0, keepdims=True)
        hit = row == idx
        sel = jnp.where(hit, 1.0, sel)
        val = jnp.where(hit, -jnp.inf, val)
    return sel


def _exp_rows(s):
    n = s.shape[1] // LANES
    part = s[:, 0:LANES]
    for c in range(1, n):
        part = jnp.maximum(part, s[:, c * LANES:(c + 1) * LANES])
    m = jnp.maximum(jnp.max(part, axis=1, keepdims=True), M_INIT)
    e = jnp.exp(s - m)
    tot = e[:, 0:LANES]
    for c in range(1, n):
        tot = tot + e[:, c * LANES:(c + 1) * LANES]
    d = jnp.sum(tot, axis=1, keepdims=True)
    return e, 1.0 / jnp.where(d > 0, d, 1.0)


def _online_step(s, v_ext, m_ref, acc_ref):
    tk = s.shape[1]
    part = s[:, 0:LANES]
    for c in range(1, tk // LANES):
        part = jnp.maximum(part, s[:, c * LANES:(c + 1) * LANES])
    m_prev = m_ref[...]
    m_new = jnp.maximum(m_prev, jnp.max(part, axis=1, keepdims=True))
    alpha = jnp.exp(m_prev - m_new)
    p = jnp.exp(s - jnp.tile(m_new, (1, tk // LANES)))
    acc_ref[...] = jnp.tile(alpha, (1, 2)) * acc_ref[...] + _dot(p.astype(BF16), v_ext)
    m_ref[...] = m_new


def _dense_masked_attention(qs, t_all, k_ref, v_ref, n_chunks, m_ref, acc_ref, sa_ref, sb_ref):
    tk = ATT_TK
    m_ref[...] = jnp.full(m_ref.shape, M_INIT, F32)
    acc_ref[...] = jnp.zeros(acc_ref.shape, F32)

    def scores(j):
        return _dot_nt(qs, k_ref[0, pl.ds(pl.multiple_of(j * tk, tk), tk), :])

    def step(s, j):
        _online_step(s, v_ref[0, pl.ds(pl.multiple_of(j * tk, tk), tk), :], m_ref, acc_ref)

    def causal(s, j):
        kpos = j * tk + lax.broadcasted_iota(jnp.int32, (1, tk), 1)
        return jnp.where(kpos <= t_all, s, MASK_BIAS)

    last = n_chunks - 1
    sa_ref[...] = scores(0)

    def body(jj, carry):
        j = 2 * jj
        sb_ref[...] = scores(j + 1)
        step(sa_ref[...], j)
        sa_ref[...] = scores(j + 2)
        step(sb_ref[...], j + 1)
        return carry

    lax.fori_loop(0, last // 2, body, 0)

    @pl.when(last % 2 == 1)
    def _():
        sb_ref[...] = scores(last)
        step(sa_ref[...], last - 1)
        step(causal(sb_ref[...], last), last)

    @pl.when(last % 2 == 0)
    def _():
        step(causal(sa_ref[...], last), last)

    acc = acc_ref[...]
    return acc[:, 0:LANES] / acc[:, LANES:2 * LANES]


def _half_select(blk, src_half, dst_half):
    if src_half != dst_half:
        blk = pltpu.roll(blk, HEAD_DIM, 1)
    lane = lax.broadcasted_iota(jnp.int32, blk.shape, 1)
    return jnp.where((lane >= dst_half * HEAD_DIM) & (lane < (dst_half + 1) * HEAD_DIM), blk, 0.0)


def _nsa_kernel(qraw_ref, qrot_ref, kc_ref, vc_ref, ks_ref, vs_ref, kw_ref, vw_ref, g_ref, z_ref,
                ovt_ref, eg_ref, y_ref, m_ref, acc_ref, sa_ref, sb_ref):
    i = pl.program_id(1)
    tq, tk = ATT_TQ, ATT_TK
    r = NSA_GROUP
    s0 = i * tq
    scale = HEAD_DIM ** -0.5
    t = s0 + lax.broadcasted_iota(jnp.int32, (tq, 1), 0)
    t_all = jnp.concatenate([t] * r, axis=0)
    lane = lax.broadcasted_iota(jnp.int32, (tq, LANES), 1)

    kc = kc_ref[0]
    n_cmp = kc.shape[0]
    kc_hi, kc_lo = _split2(kc)
    vc_bf = vc_ref[0].astype(BF16)
    cmp_end = lax.broadcasted_iota(jnp.int32, (1, n_cmp), 1) * CMP_STRIDE + (CMP_BLOCK - 1)
    cmp_bias = jnp.concatenate([jnp.where(cmp_end <= t, 0.0, MASK_BIAS)] * r, axis=0)

    win_start = jnp.maximum(s0 - WINDOW, 0)
    n_win = WINDOW + tq
    kwin = kw_ref[0, pl.ds(pl.multiple_of(win_start, tq), n_win), :]
    vwin = vw_ref[0, pl.ds(pl.multiple_of(win_start, tq), n_win), :]
    kpos_w = win_start + lax.broadcasted_iota(jnp.int32, (1, n_win), 1)
    in_window = (kpos_w <= t) & (kpos_w > t - WINDOW)
    win_bias = jnp.concatenate([jnp.where(in_window, 0.0, MASK_BIAS)] * r, axis=0)

    blk = lax.broadcasted_iota(jnp.int32, (LANES, tq), 0)
    qblk = _block_of(s0 + lax.broadcasted_iota(jnp.int32, (1, tq), 1), SEL_BLOCK)
    forced = (blk == 0) | (blk == qblk) | (blk == qblk - 1)
    ovt = ovt_ref[...]

    qraw = qraw_ref[0]
    qrot = qrot_ref[0]
    o_cmp, o_slc, o_win = [], [], []
    for g in range(NSA_KV_HEADS):
        heads = [g * r + j for j in range(r)]
        qn = jnp.concatenate(
            [_half_select(qraw[:, (h // 2) * LANES:(h // 2 + 1) * LANES], h % 2, g) for h in heads], axis=0) * scale
        qr = jnp.concatenate(
            [_half_select(qrot[:, (h // 2) * LANES:(h // 2 + 1) * LANES], h % 2, g) for h in heads], axis=0) * scale
        qr_bf = qr.astype(BF16)

        qn_hi, qn_lo = _split2(qn)
        e, rinv = _exp_rows(_dot3(qn_hi, qn_lo, kc_hi, kc_lo, dot=_dot_nt) + cmp_bias)
        pc = e * rinv
        o_cmp.append(_dot(pc.astype(BF16), vc_bf))

        pc_sum = pc[0:tq]
        for j in range(1, r):
            pc_sum = pc_sum + pc[j * tq:(j + 1) * tq]
        p_hi, p_mid, p_lo = _split3(pc_sum)
        imp = _dot_nt(ovt, p_hi) + (_dot_nt(ovt, p_mid) + _dot_nt(ovt, p_lo))
        imp = jnp.where(forced, POS_BIG, jnp.where(blk > qblk, NEG_BIG, imp))
        sel = _top_k_rows(imp, SEL_TOPK)
        sel_bias = jnp.where(sel.T > 0.5, 0.0, MASK_BIAS).astype(BF16)
        qs = jnp.concatenate([qr_bf, jnp.concatenate([sel_bias] * r, axis=0)], axis=1)

        o_slc.append(_dense_masked_attention(qs, t_all, ks_ref, vs_ref, (s0 + tq + tk - 1) // tk, m_ref, acc_ref,
                                             sa_ref, sb_ref))

        e, rinv = _exp_rows(_dot_nt(qr_bf, kwin) + win_bias)
        o_win.append(_dot(e.astype(BF16), vwin) * rinv)

    g_hi, g_lo = _split2(_sigmoid(g_ref[0]))
    eg = eg_ref[...]
    gates = _dot(g_hi, eg) + _dot(g_lo, eg)
    z = z_ref[0]
    for p in range(D_NSA // LANES):
        g = (2 * p) // r
        y = None
        for b, o_all in enumerate((o_cmp, o_slc, o_win)):
            o = o_all[g]
            j0, j1 = (2 * p) % r, (2 * p + 1) % r
            a0 = o[j0 * tq:(j0 + 1) * tq]
            a1 = o[j1 * tq:(j1 + 1) * tq]
            if g != 0:
                a0 = pltpu.roll(a0, HEAD_DIM, 1)
            if g != 1:
                a1 = pltpu.roll(a1, HEAD_DIM, 1)
            pair = jnp.where(lane < HEAD_DIM, a0, a1)
            term = gates[:, b * D_NSA + p * LANES:b * D_NSA + (p + 1) * LANES] * pair
            y = term if y is None else y + term
        zp = z[:, p * LANES:(p + 1) * LANES]
        y_ref[0, :, p * LANES:(p + 1) * LANES] = (y * (zp * _sigmoid(zp))).astype(BF16)


def _nsa_attention(qraw, qrot, kc, vc, ks_ext, vs, kw, vw, gl, z, batch, seq_len):
    tq = ATT_TQ
    n_cmp = seq_len // CMP_STRIDE
    n_sel = seq_len // SEL_BLOCK
    c_start = np.arange(n_cmp)[:, None] * CMP_STRIDE
    s_start = np.arange(LANES)[None, :] * SEL_BLOCK
    overlap = ((c_start < s_start + SEL_BLOCK) & (c_start + CMP_BLOCK > s_start) & (np.arange(LANES)[None, :] < n_sel))
    overlap_t = jnp.asarray(overlap.T.astype(np.float32), dtype=BF16)
    eg = np.zeros((LANES, N_NSA_BRANCHES * D_NSA), np.float32)
    for h in range(NSA_HEADS):
        for b in range(N_NSA_BRANCHES):
            eg[h * N_NSA_BRANCHES + b, b * D_NSA + h * HEAD_DIM:b * D_NSA + (h + 1) * HEAD_DIM] = 1.0
    eg = jnp.asarray(eg, dtype=BF16)

    def r3(a):
        return a.reshape(batch, seq_len, a.shape[-1])

    tile = lambda w: pl.BlockSpec((1, tq, w), lambda b, i: (b, i, 0))
    full = lambda n, w: pl.BlockSpec((1, n, w), lambda b, i: (b, 0, 0))
    const = lambda shape: pl.BlockSpec(shape, lambda b, i: (0, 0))
    rows = NSA_GROUP * tq
    return pl.pallas_call(
        _nsa_kernel,
        grid=(batch, seq_len // tq),
        in_specs=[tile(D_NSA), tile(D_NSA), full(n_cmp, D_NSA_KV), full(n_cmp, D_NSA_KV),
                  full(seq_len, 2 * LANES), full(seq_len, 2 * LANES), full(seq_len, D_NSA_KV),
                  full(seq_len, D_NSA_KV), tile(LANES), tile(D_NSA),
                  const(overlap_t.shape), const(eg.shape)],
        out_specs=tile(D_NSA),
        out_shape=jax.ShapeDtypeStruct((batch, seq_len, D_NSA), BF16),
        scratch_shapes=[pltpu.VMEM((rows, LANES), F32), pltpu.VMEM((rows, 2 * LANES), F32),
                        pltpu.VMEM((rows, ATT_TK), F32), pltpu.VMEM((rows, ATT_TK), F32)],
        compiler_params=pltpu.CompilerParams(dimension_semantics=("arbitrary", "arbitrary"),
                                             vmem_limit_bytes=VMEM_LIMIT),
        name="nsa_attention",
    )(r3(qraw), r3(qrot), kc, vc, r3(ks_ext), r3(vs), r3(kw), r3(vw), r3(gl), r3(z), overlap_t, eg)


def _moba_kernel(q_ref, km_ref, mk_ref, mv_ref, z_ref, y_ref, m_ref, acc_ref, sa_ref, sb_ref):
    own = pl.program_id(2)
    tq, tk = ATT_TQ, ATT_TK
    s0 = own * tq
    scale = HEAD_DIM ** -0.5
    t = s0 + lax.broadcasted_iota(jnp.int32, (tq, 1), 0)
    t_all = jnp.concatenate([t, t], axis=0)
    lane = lax.broadcasted_iota(jnp.int32, (tq, LANES), 1)

    q = q_ref[0]
    km = km_ref[0]
    nb = km.shape[0]
    if nb < LANES:
        km = jnp.concatenate([km, jnp.zeros((LANES - nb, LANES), F32)], axis=0)
    km_hi, km_lo = _split2(km)

    blk = lax.broadcasted_iota(jnp.int32, (LANES, tq), 0)
    qs = []
    for hh in range(2):
        qh = _half_select(q, hh, hh)
        q_hi, q_lo = _split2(qh)
        sg = _dot3(km_hi, km_lo, q_hi, q_lo, dot=_dot_nt)
        chosen = _top_k_rows(jnp.where(blk < own, sg, NEG_BIG), MOBA_TOPK)
        visible = jnp.where(blk < own, chosen, jnp.where(blk == own, 1.0, 0.0))
        bias = jnp.where(visible.T > 0.5, 0.0, MASK_BIAS).astype(BF16)
        qs.append(jnp.concatenate([(qh * scale).astype(BF16), bias], axis=1))
    qs = jnp.concatenate(qs, axis=0)

    o = _dense_masked_attention(qs, t_all, mk_ref, mv_ref, (s0 + tq + tk - 1) // tk, m_ref, acc_ref,
                                sa_ref, sb_ref)
    y = jnp.where(lane < HEAD_DIM, o[0:tq], o[tq:2 * tq])
    z = z_ref[0]
    y_ref[0] = (y * (z * _sigmoid(z))).astype(BF16)


def _moba_attention(mq, kmean, mk_ext, mv, mz, batch, seq_len):
    tq = ATT_TQ
    nb = seq_len // MOBA_BLOCK
    n_pairs = D_MOBA // LANES

    def r3(a):
        return a.reshape(batch, seq_len, a.shape[-1])

    return pl.pallas_call(
        _moba_kernel,
        grid=(batch, n_pairs, seq_len // tq),
        in_specs=[pl.BlockSpec((1, tq, LANES), lambda b, p, i: (b, i, p)),
                  pl.BlockSpec((1, nb, LANES), lambda b, p, i: (b, 0, p)),
                  pl.BlockSpec((1, seq_len, 2 * LANES), lambda b, p, i: (b, 0, p)),
                  pl.BlockSpec((1, seq_len, 2 * LANES), lambda b, p, i: (b, 0, p)),
                  pl.BlockSpec((1, tq, LANES), lambda b, p, i: (b, i, p))],
        out_specs=pl.BlockSpec((1, tq, LANES), lambda b, p, i: (b, i, p)),
        out_shape=jax.ShapeDtypeStruct((batch, seq_len, D_MOBA), BF16),
        scratch_shapes=[pltpu.VMEM((2 * tq, LANES), F32), pltpu.VMEM((2 * tq, 2 * LANES), F32),
                        pltpu.VMEM((2 * tq, ATT_TK), F32), pltpu.VMEM((2 * tq, ATT_TK), F32)],
        compiler_params=pltpu.CompilerParams(dimension_semantics=("arbitrary", "arbitrary", "arbitrary"),
                                             vmem_limit_bytes=VMEM_LIMIT),
        name="moba_attention",
    )(r3(mq), kmean.reshape(batch, nb, D_MOBA), r3(mk_ext), r3(mv), r3(mz))


def _out_kernel(x_ref, yn_ref, ym_ref, w_ref, gain_ref, bias_ref, o_ref, wbf_ref):
    @pl.when(pl.program_id(0) == 0)
    def _():
        wbf_ref[...] = w_ref[...].astype(BF16)

    sub = _dot(yn_ref[...], wbf_ref[0:D_NSA, :]) + _dot(ym_ref[...], wbf_ref[D_NSA:D_NSA + D_MOBA, :])
    h = DEEPNORM_ALPHA * x_ref[...] + sub
    mu = jnp.mean(h, axis=1, keepdims=True)
    d = h - mu
    var = jnp.mean(d * d, axis=1, keepdims=True)
    o_ref[...] = d * lax.rsqrt(var + LN_EPS) * gain_ref[...] + bias_ref[...]


def _out_project(x2, y_nsa, y_moba, w_out, gain, bias):
    m = x2.shape[0]
    tm = PROJ_TM
    d_mix = D_NSA + D_MOBA
    return pl.pallas_call(
        _out_kernel,
        grid=(m // tm,),
        in_specs=[_row_spec(tm, D_MODEL), _row_spec(tm, D_NSA), _row_spec(tm, D_MOBA),
                  pl.BlockSpec((d_mix, D_MODEL), lambda i: (0, 0)),
                  pl.BlockSpec((1, D_MODEL), lambda i: (0, 0)), pl.BlockSpec((1, D_MODEL), lambda i: (0, 0))],
        out_specs=_row_spec(tm, D_MODEL),
        out_shape=jax.ShapeDtypeStruct((m, D_MODEL), F32),
        scratch_shapes=[pltpu.VMEM((d_mix, D_MODEL), BF16)],
        compiler_params=pltpu.CompilerParams(dimension_semantics=("arbitrary",), vmem_limit_bytes=VMEM_LIMIT),
        name="out_proj_norm",
    )(x2, y_nsa, y_moba, w_out, gain.reshape(1, D_MODEL), bias.reshape(1, D_MODEL))


def _rope_tables(seq_len):
    half = ROPE_DIM // 2
    inv_freq = ROPE_THETA ** (-jnp.arange(0, ROPE_DIM, 2, dtype=F32) / ROPE_DIM)
    ang = jnp.arange(seq_len, dtype=F32)[:, None] * inv_freq[None, :]
    cos, sin = jnp.cos(ang), jnp.sin(ang)
    zeros = jnp.zeros((seq_len, HEAD_DIM - ROPE_DIM), F32)
    zh = jnp.zeros((seq_len, half), F32)
    c = jnp.concatenate([cos, cos, zeros + 1.0], axis=1)
    sm = jnp.concatenate([-sin, zh, zeros], axis=1)
    sp = jnp.concatenate([zh, sin, zeros], axis=1)
    return tuple(jnp.tile(a, (1, LANES // HEAD_DIM)) for a in (c, sm, sp))


def _layer(x, w_in, cmp_pos_k, cmp_w1_k, cmp_b1_k, cmp_w2_k, cmp_pos_v, cmp_w1_v, cmp_b1_v, cmp_w2_v,
           w_out, ln_gain, ln_bias, tables):
    batch, seq_len, _ = x.shape
    kv = D_NSA_KV
    sizes = (D_NSA, kv, kv, kv, kv, kv, kv, N_NSA_BRANCHES * NSA_HEADS, D_NSA, D_MOBA, D_MOBA, D_MOBA, D_MOBA)
    offs = np.concatenate([[0], np.cumsum(sizes)])
    col = lambda s: w_in[:, int(offs[s]):int(offs[s + 1])]
    w_prec = jnp.concatenate([col(0), col(1), col(9), col(10)], axis=1)
    gate_pad = jnp.zeros((D_MODEL, LANES - sizes[7]), F32)
    w_smooth = jnp.concatenate([col(2), col(3), col(4), col(5), col(6), col(7), gate_pad,
                                col(8), col(11), col(12)], axis=1)

    x2 = x.reshape(batch * seq_len, D_MODEL)
    (qraw, qrot, kc_raw, mq, mk_ext, kmean), (vc_raw, ks_ext, vs, kw, vw, gl, z, mv, mz) = _project(
        x2, w_prec, w_smooth, tables, seq_len)
    kc = _compress(kc_raw, cmp_pos_k, cmp_w1_k, cmp_b1_k, cmp_w2_k, batch, seq_len)
    vc = _compress(vc_raw, cmp_pos_v, cmp_w1_v, cmp_b1_v, cmp_w2_v, batch, seq_len)
    y_nsa = _nsa_attention(qraw, qrot, kc, vc, ks_ext, vs, kw, vw, gl, z, batch, seq_len)
    y_moba = _moba_attention(mq, kmean, mk_ext, mv, mz, batch, seq_len)
    out = _out_project(x2, y_nsa.reshape(batch * seq_len, D_NSA), y_moba.reshape(batch * seq_len, D_MOBA),
                       w_out, ln_gain, ln_bias)
    return out.reshape(batch, seq_len, D_MODEL)


def kernel(x, w_in, cmp_pos_k, cmp_w1_k, cmp_b1_k, cmp_w2_k, cmp_pos_v, cmp_w1_v, cmp_b1_v, cmp_w2_v,
           w_out, ln_gain, ln_bias):
    tables = _rope_tables(x.shape[1])
    h = x
    for layer in range(w_in.shape[0]):
        h = _layer(h, w_in[layer], cmp_pos_k[layer], cmp_w1_k[layer], cmp_b1_k[layer], cmp_w2_k[layer],
                   cmp_pos_v[layer], cmp_w1_v[layer], cmp_b1_v[layer], cmp_w2_v[layer],
                   w_out[layer], ln_gain[layer], ln_bias[layer], tables)
    return h
```

```python
import functools

import numpy as np
import jax
import jax.numpy as jnp
from jax import lax
from jax.experimental import pallas as pl
from jax.experimental.pallas import tpu as pltpu

D_MODEL = 1024
HEAD_DIM = 64
NSA_HEADS = 8
NSA_KV_HEADS = 2
NSA_GROUP = NSA_HEADS // NSA_KV_HEADS
MOBA_HEADS = 8
D_NSA = NSA_HEADS * HEAD_DIM
D_NSA_KV = NSA_KV_HEADS * HEAD_DIM
D_MOBA = MOBA_HEADS * HEAD_DIM
N_NSA_BRANCHES = 3
ROPE_THETA = 500000.0
ROPE_DIM = HEAD_DIM // 4
CMP_BLOCK = 32
CMP_STRIDE = 16
CMP_HIDDEN = 4 * HEAD_DIM
SEL_BLOCK = 64
SEL_TOPK = 16
N_FORCED = 3
WINDOW = 512
MOBA_BLOCK = 256
MOBA_TOPK = 3
LN_EPS = 1e-5
DEPTH = 1
DEEPNORM_ALPHA = (2.0 * DEPTH) ** 0.25
NEG_BIG = -1e30
POS_BIG = 1e30
MASK_BIAS = -1e30
M_INIT = -1e29
LOG2E = 1.4426950408889634
Q_SCALE = HEAD_DIM ** -0.5 * LOG2E

LANES = 128
PROJ_TM = 512
ATT_TQ = 256
MOBA_TQ = 512
ATT_TK = 512
VMEM_LIMIT = 52 * 1024 * 1024

F32 = jnp.float32
BF16 = jnp.bfloat16
NT_DIMS = (((1,), (1,)), ((), ()))


def _dot(a, b):
    return jnp.dot(a, b, preferred_element_type=F32)


def _dot_nt(a, b):
    return lax.dot_general(a, b, NT_DIMS, preferred_element_type=F32)


def _rope(a, c, sm, sp):
    outs = []
    for p in range(a.shape[1] // LANES):
        s = a[:, p * LANES:(p + 1) * LANES]
        outs.append(s * c + pltpu.roll(s, LANES - ROPE_DIM // 2, 1) * sm + pltpu.roll(s, ROPE_DIM // 2, 1) * sp)
    return outs[0] if len(outs) == 1 else jnp.concatenate(outs, axis=1)


def _sigmoid(x):
    return 1.0 / (1.0 + jnp.exp(-x))


def _block_of(t, block):
    assert block & (block - 1) == 0
    return lax.shift_right_logical(t, jnp.int32(block.bit_length() - 1))


def _proj_qk_kernel(x_ref, w_ref, c_ref, sm_ref, sp_ref,
                    qn_ref, qr_ref, kc_ref, mq_ref, mk_ref, km_ref, wbf_ref, *, seq_len):
    i = pl.program_id(0)
    tm = x_ref.shape[0]

    @pl.when(i == 0)
    def _():
        wbf_ref[...] = w_ref[...].astype(BF16)

    acc = _dot(x_ref[...].astype(BF16), wbf_ref[...])
    c, sm, sp = c_ref[...], sm_ref[...], sp_ref[...]
    scale = Q_SCALE
    nq = NSA_HEADS * LANES

    q = acc[:, 0:nq] * scale
    qn_ref[...] = q.astype(BF16)
    qr_ref[...] = _rope(q, c, sm, sp).astype(BF16)
    kc_ref[...] = acc[:, nq:nq + D_NSA_KV]
    o = nq + D_NSA_KV
    mq_ref[...] = _rope(acc[:, o:o + D_MOBA] * scale, c, sm, sp).astype(BF16)
    mk = _rope(acc[:, o + D_MOBA:o + 2 * D_MOBA], c, sm, sp)

    t = (i * tm) % seq_len + lax.broadcasted_iota(jnp.int32, (tm, LANES), 0)
    lane = lax.broadcasted_iota(jnp.int32, (tm, LANES), 1)
    onehot = jnp.where(lane == _block_of(t, MOBA_BLOCK), 1.0, 0.0).astype(BF16)
    for p in range(D_MOBA // LANES):
        mk_ref[:, 2 * p * LANES:(2 * p + 1) * LANES] = mk[:, p * LANES:(p + 1) * LANES].astype(BF16)
        mk_ref[:, (2 * p + 1) * LANES:(2 * p + 2) * LANES] = onehot
    for r in range(tm // MOBA_BLOCK):
        blk = mk[r * MOBA_BLOCK:(r + 1) * MOBA_BLOCK]
        km_ref[r] = jnp.sum(blk, axis=0, keepdims=True) / MOBA_BLOCK


def _proj_smooth_kernel(x_ref, w_ref, c_ref, sm_ref, sp_ref,
                        vc_ref, ks_ref, vs_ref, kw_ref, vw_ref, g_ref, z_ref, mv_ref, mz_ref,
                        wbf_ref, *, seq_len):
    i = pl.program_id(0)
    tm = x_ref.shape[0]

    @pl.when(i == 0)
    def _():
        wbf_ref[...] = w_ref[...].astype(BF16)

    acc = _dot(x_ref[...].astype(BF16), wbf_ref[...])
    c, sm, sp = c_ref[...], sm_ref[...], sp_ref[...]
    kv = D_NSA_KV
    vc_ref[...] = acc[:, 0:kv]
    t = (i * tm) % seq_len + lax.broadcasted_iota(jnp.int32, (tm, LANES), 0)
    lane = lax.broadcasted_iota(jnp.int32, (tm, LANES), 1)
    ks_ref[:, 0:LANES] = _rope(acc[:, kv:2 * kv], c, sm, sp).astype(BF16)
    ks_ref[:, LANES:2 * LANES] = jnp.where(lane == _block_of(t, SEL_BLOCK), 1.0, 0.0).astype(BF16)
    ones = jnp.ones((tm, LANES), BF16)
    vs_ref[:, 0:LANES] = acc[:, 2 * kv:3 * kv].astype(BF16)
    vs_ref[:, LANES:2 * LANES] = ones
    kw_ref[...] = _rope(acc[:, 3 * kv:4 * kv], c, sm, sp).astype(BF16)
    vw_ref[...] = acc[:, 4 * kv:5 * kv].astype(BF16)
    g_ref[...] = acc[:, 5 * kv:6 * kv]
    o = 6 * kv
    z_ref[...] = acc[:, o:o + D_NSA]
    mv = acc[:, o + D_NSA:o + D_NSA + D_MOBA].astype(BF16)
    for p in range(D_MOBA // LANES):
        mv_ref[:, 2 * p * LANES:(2 * p + 1) * LANES] = mv[:, p * LANES:(p + 1) * LANES]
        mv_ref[:, (2 * p + 1) * LANES:(2 * p + 2) * LANES] = ones
    mz_ref[...] = acc[:, o + D_NSA + D_MOBA:o + D_NSA + 2 * D_MOBA]


def _row_spec(tm, width):
    return pl.BlockSpec((tm, width), lambda i: (i, 0))


def _table_spec(tm, seq_len):
    nt = seq_len // tm
    return pl.BlockSpec((tm, LANES), lambda i: (i % nt, 0))


def _project(x2, w_qk, w_smooth, tables, seq_len):
    m = x2.shape[0]
    tm = PROJ_TM
    grid = (m // tm,)
    n_qk, n_smooth = w_qk.shape[1], w_smooth.shape[1]
    nq = NSA_HEADS * LANES
    params = pltpu.CompilerParams(dimension_semantics=("arbitrary",), vmem_limit_bytes=VMEM_LIMIT)
    tab_specs = [_table_spec(tm, seq_len)] * 3

    qk = pl.pallas_call(
        functools.partial(_proj_qk_kernel, seq_len=seq_len),
        grid=grid,
        in_specs=[_row_spec(tm, D_MODEL), pl.BlockSpec((D_MODEL, n_qk), lambda i: (0, 0))] + tab_specs,
        out_specs=[_row_spec(tm, nq), _row_spec(tm, nq), _row_spec(tm, D_NSA_KV), _row_spec(tm, D_MOBA),
                   _row_spec(tm, 2 * D_MOBA),
                   pl.BlockSpec((tm // MOBA_BLOCK, 1, D_MOBA), lambda i: (i, 0, 0))],
        out_shape=[jax.ShapeDtypeStruct((m, nq), BF16), jax.ShapeDtypeStruct((m, nq), BF16),
                   jax.ShapeDtypeStruct((m, D_NSA_KV), F32), jax.ShapeDtypeStruct((m, D_MOBA), BF16),
                   jax.ShapeDtypeStruct((m, 2 * D_MOBA), BF16),
                   jax.ShapeDtypeStruct((m // MOBA_BLOCK, 1, D_MOBA), F32)],
        scratch_shapes=[pltpu.VMEM((D_MODEL, n_qk), BF16)],
        compiler_params=params,
        name="proj_qk",
    )(x2, w_qk, *tables)

    smooth = pl.pallas_call(
        functools.partial(_proj_smooth_kernel, seq_len=seq_len),
        grid=grid,
        in_specs=[_row_spec(tm, D_MODEL), pl.BlockSpec((D_MODEL, n_smooth), lambda i: (0, 0))] + tab_specs,
        out_specs=[_row_spec(tm, D_NSA_KV), _row_spec(tm, 2 * LANES), _row_spec(tm, 2 * LANES),
                   _row_spec(tm, D_NSA_KV), _row_spec(tm, D_NSA_KV), _row_spec(tm, LANES),
                   _row_spec(tm, D_NSA), _row_spec(tm, 2 * D_MOBA), _row_spec(tm, D_MOBA)],
        out_shape=[jax.ShapeDtypeStruct((m, D_NSA_KV), F32), jax.ShapeDtypeStruct((m, 2 * LANES), BF16),
                   jax.ShapeDtypeStruct((m, 2 * LANES), BF16), jax.ShapeDtypeStruct((m, D_NSA_KV), BF16),
                   jax.ShapeDtypeStruct((m, D_NSA_KV), BF16), jax.ShapeDtypeStruct((m, LANES), F32),
                   jax.ShapeDtypeStruct((m, D_NSA), F32), jax.ShapeDtypeStruct((m, 2 * D_MOBA), BF16),
                   jax.ShapeDtypeStruct((m, D_MOBA), F32)],
        scratch_shapes=[pltpu.VMEM((D_MODEL, n_smooth), BF16)],
        compiler_params=params,
        name="proj_smooth",
    )(x2, w_smooth, *tables)
    return qk, smooth


def _compress_kernel(ch_ref, pos_ref, w1_ref, b1_ref, w2_ref, out_ref):
    ch = ch_ref[0]
    n = ch.shape[0]
    parts = []
    for half in range(CMP_BLOCK // CMP_STRIDE):
        parts.append(_dot((ch + pos_ref[half]).astype(BF16), w1_ref[half].astype(BF16)))
    nxt = pltpu.roll(parts[1], n - 1, 0)
    row = lax.broadcasted_iota(jnp.int32, nxt.shape, 0)
    h = parts[0] + jnp.where(row < n - 1, nxt, 0.0) + b1_ref[...]
    h = h * _sigmoid(h)
    out_ref[0] = _dot(h.astype(BF16), w2_ref[...].astype(BF16))


def _compress(raw, pos, w1, b1, w2, batch, seq_len):
    g = NSA_KV_HEADS
    n_chunks = seq_len // CMP_STRIDE
    feat = CMP_STRIDE * D_NSA_KV
    ch = raw.reshape(batch, n_chunks, feat)
    n_half = CMP_BLOCK // CMP_STRIDE
    eye = jnp.eye(g, dtype=F32)
    w1r = w1.reshape(n_half, CMP_STRIDE, HEAD_DIM, CMP_HIDDEN)
    w1b = jnp.einsum('aldh,gk->algdkh', w1r, eye).reshape(n_half, feat, g * CMP_HIDDEN)
    posb = jnp.broadcast_to(pos.reshape(n_half, CMP_STRIDE, 1, HEAD_DIM),
                            (n_half, CMP_STRIDE, g, HEAD_DIM)).reshape(n_half, 1, feat)
    b1b = jnp.tile(b1.reshape(1, CMP_HIDDEN), (1, g))
    w2b = jnp.einsum('hd,gk->ghkd', w2, eye).reshape(g * CMP_HIDDEN, g * HEAD_DIM)
    return pl.pallas_call(
        _compress_kernel,
        grid=(batch,),
        in_specs=[pl.BlockSpec((1, n_chunks, feat), lambda b: (b, 0, 0)),
                  pl.BlockSpec((n_half, 1, feat), lambda b: (0, 0, 0)),
                  pl.BlockSpec((n_half, feat, g * CMP_HIDDEN), lambda b: (0, 0, 0)),
                  pl.BlockSpec((1, g * CMP_HIDDEN), lambda b: (0, 0)),
                  pl.BlockSpec((g * CMP_HIDDEN, g * HEAD_DIM), lambda b: (0, 0))],
        out_specs=pl.BlockSpec((1, n_chunks, g * HEAD_DIM), lambda b: (b, 0, 0)),
        out_shape=jax.ShapeDtypeStruct((batch, n_chunks, g * HEAD_DIM), F32),
        compiler_params=pltpu.CompilerParams(dimension_semantics=("arbitrary",), vmem_limit_bytes=VMEM_LIMIT),
        name="compress",
    )(ch, posb, w1b, b1b, w2b)


def _top_k_rows(val, k, sel=None):
    row = lax.broadcasted_iota(jnp.int32, val.shape, 0).astype(F32)
    sel = jnp.zeros(val.shape, F32) if sel is None else sel
    for _ in range(k):
        m = jnp.max(val, axis=0, keepdims=True)
        idx = jnp.min(jnp.where(val == m, row, float(LANES)), axis=0, keepdims=True)
        hit = row == idx
        sel = jnp.where(hit, 1.0, sel)
        val = jnp.where(hit, -jnp.inf, val)
    return sel


def _exp2_rows(s, v_ext):
    n = s.shape[1] // LANES
    part = s[:, 0:LANES]
    for c in range(1, n):
        part = jnp.maximum(part, s[:, c * LANES:(c + 1) * LANES])
    m = jnp.maximum(jnp.max(part, axis=1, keepdims=True), M_INIT)
    e = jnp.exp2(s - m)
    ov = _dot(e.astype(BF16), v_ext)
    d = ov[:, LANES:2 * LANES]
    return e, ov[:, 0:LANES], 1.0 / jnp.where(d > 0, d, 1.0)


def _online_step(s, v_ext, m_ref, acc_ref):
    tk = s.shape[1]
    part = s[:, 0:LANES]
    for c in range(1, tk // LANES):
        part = jnp.maximum(part, s[:, c * LANES:(c + 1) * LANES])
    m_prev = m_ref[...]
    m_new = jnp.maximum(m_prev, jnp.max(part, axis=1, keepdims=True))
    alpha = jnp.exp2(m_prev - m_new)
    p = jnp.exp2(s - jnp.tile(m_new, (1, tk // LANES)))
    acc_ref[...] = jnp.tile(alpha, (1, 2)) * acc_ref[...] + _dot(p.astype(BF16), v_ext)
    m_ref[...] = m_new


def _dense_masked_attention(qs, t_all, k_ref, v_ref, n_chunks, m_ref, acc_ref, sa_ref, sb_ref):
    tk = ATT_TK
    m_ref[...] = jnp.full(m_ref.shape, M_INIT, F32)
    acc_ref[...] = jnp.zeros(acc_ref.shape, F32)

    def scores(j):
        return _dot_nt(qs, k_ref[0, pl.ds(pl.multiple_of(j * tk, tk), tk), :])

    def step(s, j):
        _online_step(s, v_ref[0, pl.ds(pl.multiple_of(j * tk, tk), tk), :], m_ref, acc_ref)

    def causal(s, j):
        kpos = j * tk + lax.broadcasted_iota(jnp.int32, (1, tk), 1)
        return jnp.where(kpos <= t_all, s, MASK_BIAS)

    last = n_chunks - 1
    sa_ref[...] = scores(0)

    def body(jj, carry):
        j = 2 * jj
        sb_ref[...] = scores(j + 1)
        step(sa_ref[...], j)
        sa_ref[...] = scores(j + 2)
        step(sb_ref[...], j + 1)
        return carry

    lax.fori_loop(0, last // 2, body, 0)

    @pl.when(last % 2 == 1)
    def _():
        sb_ref[...] = scores(last)
        step(sa_ref[...], last - 1)
        step(causal(sb_ref[...], last), last)

    @pl.when(last % 2 == 0)
    def _():
        step(causal(sa_ref[...], last), last)

    acc = acc_ref[...]
    return acc[:, 0:LANES] / acc[:, LANES:2 * LANES]


def _nsa_kernel(qn_ref, qr_ref, kc_ref, vc_ref, ks_ref, vs_ref, kw_ref, vw_ref, g_ref, z_ref,
                ovt_ref, eg_ref, y_ref, m_ref, acc_ref, sa_ref, sb_ref):
    i = pl.program_id(1)
    tq, tk = ATT_TQ, ATT_TK
    r = NSA_GROUP
    s0 = i * tq
    t = s0 + lax.broadcasted_iota(jnp.int32, (tq, 1), 0)
    t_all = jnp.concatenate([t] * r, axis=0)
    lane = lax.broadcasted_iota(jnp.int32, (tq, LANES), 1)

    def per_head_bias(s, bias):
        return jnp.concatenate([s[j * tq:(j + 1) * tq] + bias for j in range(r)], axis=0)

    kc_bf = kc_ref[0].astype(BF16)
    n_cmp = kc_bf.shape[0]
    vc_ext = jnp.concatenate([vc_ref[0].astype(BF16), jnp.ones((n_cmp, LANES), BF16)], axis=1)
    cmp_end = lax.broadcasted_iota(jnp.int32, (1, n_cmp), 1) * CMP_STRIDE + (CMP_BLOCK - 1)
    cmp_bias = jnp.where(cmp_end <= t, 0.0, MASK_BIAS)

    win_start = jnp.maximum(s0 - WINDOW, 0)
    n_win = WINDOW + tq
    kwin = kw_ref[0, pl.ds(pl.multiple_of(win_start, tq), n_win), :]
    vwin_ext = jnp.concatenate([vw_ref[0, pl.ds(pl.multiple_of(win_start, tq), n_win), :],
                                jnp.ones((n_win, LANES), BF16)], axis=1)
    kpos_w = win_start + lax.broadcasted_iota(jnp.int32, (1, n_win), 1)
    win_bias = jnp.where((kpos_w <= t) & (kpos_w > t - WINDOW), 0.0, MASK_BIAS)

    blk = lax.broadcasted_iota(jnp.int32, (LANES, tq), 0)
    qblk = _block_of(s0 + lax.broadcasted_iota(jnp.int32, (1, tq), 1), SEL_BLOCK)
    forced = (blk == 0) | (blk == qblk) | (blk == qblk - 1)
    forced_sel = jnp.where(forced, 1.0, 0.0)
    ovt = ovt_ref[...]

    o_cmp, o_slc, o_win = [], [], []
    for g in range(NSA_KV_HEADS):
        heads = [g * r + j for j in range(r)]
        qn = jnp.concatenate([qn_ref[0, :, h * LANES:(h + 1) * LANES] for h in heads], axis=0)
        qr = jnp.concatenate([qr_ref[0, :, h * LANES:(h + 1) * LANES] for h in heads], axis=0)

        e, ov, rinv = _exp2_rows(per_head_bias(_dot_nt(qn, kc_bf), cmp_bias), vc_ext)
        pc = e * jnp.tile(rinv, (1, n_cmp // LANES))
        o_cmp.append(ov * rinv)

        pc_sum = pc[0:tq]
        for j in range(1, r):
            pc_sum = pc_sum + pc[j * tq:(j + 1) * tq]
        imp = _dot_nt(ovt, pc_sum.astype(BF16))
        imp = jnp.where(forced, -jnp.inf, jnp.where(blk > qblk, NEG_BIG, imp))
        sel = _top_k_rows(imp, SEL_TOPK - N_FORCED, forced_sel)
        sel_bias = jnp.where(sel.T > 0.5, 0.0, MASK_BIAS).astype(BF16)
        qs = jnp.concatenate([qr, jnp.concatenate([sel_bias] * r, axis=0)], axis=1)

        _, ov, rinv = _exp2_rows(per_head_bias(_dot_nt(qr, kwin), win_bias), vwin_ext)
        o_win.append(ov * rinv)

        o_slc.append(_dense_masked_attention(qs, t_all, ks_ref, vs_ref, (s0 + tq + tk - 1) // tk, m_ref, acc_ref,
                                             sa_ref, sb_ref))

    gates = _dot(_sigmoid(g_ref[0]).astype(BF16), eg_ref[...])
    z = z_ref[0]
    for p in range(D_NSA // LANES):
        g = (2 * p) // r
        y = None
        for b, o_all in enumerate((o_cmp, o_slc, o_win)):
            o = o_all[g]
            j0, j1 = (2 * p) % r, (2 * p + 1) % r
            a0 = o[j0 * tq:(j0 + 1) * tq]
            a1 = o[j1 * tq:(j1 + 1) * tq]
            if g != 0:
                a0 = pltpu.roll(a0, HEAD_DIM, 1)
            if g != 1:
                a1 = pltpu.roll(a1, HEAD_DIM, 1)
            pair = jnp.where(lane < HEAD_DIM, a0, a1)
            term = gates[:, b * D_NSA + p * LANES:b * D_NSA + (p + 1) * LANES] * pair
            y = term if y is None else y + term
        zp = z[:, p * LANES:(p + 1) * LANES]
        y_ref[0, :, p * LANES:(p + 1) * LANES] = (y * (zp * _sigmoid(zp))).astype(BF16)


def _nsa_attention(qn, qr, kc, vc, ks_ext, vs, kw, vw, gl, z, batch, seq_len):
    tq = ATT_TQ
    n_cmp = seq_len // CMP_STRIDE
    n_sel = seq_len // SEL_BLOCK
    c_start = np.arange(n_cmp)[:, None] * CMP_STRIDE
    s_start = np.arange(LANES)[None, :] * SEL_BLOCK
    overlap = ((c_start < s_start + SEL_BLOCK) & (c_start + CMP_BLOCK > s_start) & (np.arange(LANES)[None, :] < n_sel))
    overlap_t = jnp.asarray(overlap.T.astype(np.float32), dtype=BF16)
    eg = np.zeros((LANES, N_NSA_BRANCHES * D_NSA), np.float32)
    for h in range(NSA_HEADS):
        for b in range(N_NSA_BRANCHES):
            eg[h * N_NSA_BRANCHES + b, b * D_NSA + h * HEAD_DIM:b * D_NSA + (h + 1) * HEAD_DIM] = 1.0
    eg = jnp.asarray(eg, dtype=BF16)

    def r3(a):
        return a.reshape(batch, seq_len, a.shape[-1])

    tile = lambda w: pl.BlockSpec((1, tq, w), lambda b, i: (b, i, 0))
    full = lambda n, w: pl.BlockSpec((1, n, w), lambda b, i: (b, 0, 0))
    const = lambda shape: pl.BlockSpec(shape, lambda b, i: (0, 0))
    rows = NSA_GROUP * tq
    return pl.pallas_call(
        _nsa_kernel,
        grid=(batch, seq_len // tq),
        in_specs=[tile(NSA_HEADS * LANES), tile(NSA_HEADS * LANES), full(n_cmp, D_NSA_KV), full(n_cmp, D_NSA_KV),
                  full(seq_len, 2 * LANES), full(seq_len, 2 * LANES), full(seq_len, D_NSA_KV),
                  full(seq_len, D_NSA_KV), tile(LANES), tile(D_NSA),
                  const(overlap_t.shape), const(eg.shape)],
        out_specs=tile(D_NSA),
        out_shape=jax.ShapeDtypeStruct((batch, seq_len, D_NSA), BF16),
        scratch_shapes=[pltpu.VMEM((rows, LANES), F32), pltpu.VMEM((rows, 2 * LANES), F32),
                        pltpu.VMEM((rows, ATT_TK), F32), pltpu.VMEM((rows, ATT_TK), F32)],
        compiler_params=pltpu.CompilerParams(dimension_semantics=("arbitrary", "arbitrary"),
                                             vmem_limit_bytes=VMEM_LIMIT),
        name="nsa_attention",
    )(r3(qn), r3(qr), kc, vc, r3(ks_ext), r3(vs), r3(kw), r3(vw), r3(gl), r3(z), overlap_t, eg)


def _moba_kernel(q_ref, km_ref, mk_ref, mv_ref, z_ref, y_ref, m_ref, acc_ref, sa_ref, sb_ref):
    i = pl.program_id(2)
    tq, tk = MOBA_TQ, ATT_TK
    s0 = i * tq
    t = s0 + lax.broadcasted_iota(jnp.int32, (tq, 1), 0)
    t_all = jnp.concatenate([t, t], axis=0)
    lane = lax.broadcasted_iota(jnp.int32, (tq, LANES), 1)

    q = q_ref[0]
    km = km_ref[0]
    nb = km.shape[0]
    if nb < LANES:
        km = jnp.concatenate([km, jnp.zeros((LANES - nb, LANES), F32)], axis=0)
    km_bf = km.astype(BF16)

    blk = lax.broadcasted_iota(jnp.int32, (LANES, tq), 0)
    own = _block_of(s0 + lax.broadcasted_iota(jnp.int32, (1, tq), 1), MOBA_BLOCK)
    qs = []
    for hh in range(2):
        qh = jnp.where((lane >= hh * HEAD_DIM) & (lane < (hh + 1) * HEAD_DIM), q, jnp.zeros_like(q))
        sg = _dot_nt(km_bf, qh)
        chosen = _top_k_rows(jnp.where(blk < own, sg, NEG_BIG), MOBA_TOPK)
        visible = jnp.where(blk < own, chosen, jnp.where(blk == own, 1.0, 0.0))
        bias = jnp.where(visible.T > 0.5, 0.0, MASK_BIAS).astype(BF16)
        qs.append(jnp.concatenate([qh, bias], axis=1))
    qs = jnp.concatenate(qs, axis=0)

    o = _dense_masked_attention(qs, t_all, mk_ref, mv_ref, (s0 + tq + tk - 1) // tk, m_ref, acc_ref,
                                sa_ref, sb_ref)
    y = jnp.where(lane < HEAD_DIM, o[0:tq], o[tq:2 * tq])
    z = z_ref[0]
    y_ref[0] = (y * (z * _sigmoid(z))).astype(BF16)


def _moba_attention(mq, kmean, mk_ext, mv, mz, batch, seq_len):
    tq = MOBA_TQ
    nb = seq_len // MOBA_BLOCK
    n_pairs = D_MOBA // LANES

    def r3(a):
        return a.reshape(batch, seq_len, a.shape[-1])

    return pl.pallas_call(
        _moba_kernel,
        grid=(batch, n_pairs, seq_len // tq),
        in_specs=[pl.BlockSpec((1, tq, LANES), lambda b, p, i: (b, i, p)),
                  pl.BlockSpec((1, nb, LANES), lambda b, p, i: (b, 0, p)),
                  pl.BlockSpec((1, seq_len, 2 * LANES), lambda b, p, i: (b, 0, p)),
                  pl.BlockSpec((1, seq_len, 2 * LANES), lambda b, p, i: (b, 0, p)),
                  pl.BlockSpec((1, tq, LANES), lambda b, p, i: (b, i, p))],
        out_specs=pl.BlockSpec((1, tq, LANES), lambda b, p, i: (b, i, p)),
        out_shape=jax.ShapeDtypeStruct((batch, seq_len, D_MOBA), BF16),
        scratch_shapes=[pltpu.VMEM((2 * tq, LANES), F32), pltpu.VMEM((2 * tq, 2 * LANES), F32),
                        pltpu.VMEM((2 * tq, ATT_TK), F32), pltpu.VMEM((2 * tq, ATT_TK), F32)],
        compiler_params=pltpu.CompilerParams(dimension_semantics=("arbitrary", "arbitrary", "arbitrary"),
                                             vmem_limit_bytes=VMEM_LIMIT),
        name="moba_attention",
    )(r3(mq), kmean.reshape(batch, nb, D_MOBA), r3(mk_ext), r3(mv), r3(mz))


def _out_kernel(x_ref, yn_ref, ym_ref, w_ref, gain_ref, bias_ref, o_ref, wbf_ref):
    @pl.when(pl.program_id(0) == 0)
    def _():
        wbf_ref[...] = w_ref[...].astype(BF16)

    sub = _dot(yn_ref[...], wbf_ref[0:D_NSA, :]) + _dot(ym_ref[...], wbf_ref[D_NSA:D_NSA + D_MOBA, :])
    h = DEEPNORM_ALPHA * x_ref[...] + sub
    mu = jnp.mean(h, axis=1, keepdims=True)
    d = h - mu
    var = jnp.mean(d * d, axis=1, keepdims=True)
    o_ref[...] = d * lax.rsqrt(var + LN_EPS) * gain_ref[...] + bias_ref[...]


def _out_project(x2, y_nsa, y_moba, w_out, gain, bias):
    m = x2.shape[0]
    tm = PROJ_TM
    d_mix = D_NSA + D_MOBA
    return pl.pallas_call(
        _out_kernel,
        grid=(m // tm,),
        in_specs=[_row_spec(tm, D_MODEL), _row_spec(tm, D_NSA), _row_spec(tm, D_MOBA),
                  pl.BlockSpec((d_mix, D_MODEL), lambda i: (0, 0)),
                  pl.BlockSpec((1, D_MODEL), lambda i: (0, 0)), pl.BlockSpec((1, D_MODEL), lambda i: (0, 0))],
        out_specs=_row_spec(tm, D_MODEL),
        out_shape=jax.ShapeDtypeStruct((m, D_MODEL), F32),
        scratch_shapes=[pltpu.VMEM((d_mix, D_MODEL), BF16)],
        compiler_params=pltpu.CompilerParams(dimension_semantics=("arbitrary",), vmem_limit_bytes=VMEM_LIMIT),
        name="out_proj_norm",
    )(x2, y_nsa, y_moba, w_out, gain.reshape(1, D_MODEL), bias.reshape(1, D_MODEL))


def _rope_tables(seq_len):
    half = ROPE_DIM // 2
    inv_freq = ROPE_THETA ** (-jnp.arange(0, ROPE_DIM, 2, dtype=F32) / ROPE_DIM)
    ang = jnp.arange(seq_len, dtype=F32)[:, None] * inv_freq[None, :]
    cos, sin = jnp.cos(ang), jnp.sin(ang)
    zeros = jnp.zeros((seq_len, HEAD_DIM - ROPE_DIM), F32)
    zh = jnp.zeros((seq_len, half), F32)
    c = jnp.concatenate([cos, cos, zeros + 1.0], axis=1)
    sm = jnp.concatenate([-sin, zh, zeros], axis=1)
    sp = jnp.concatenate([zh, sin, zeros], axis=1)
    return tuple(jnp.tile(a, (1, LANES // HEAD_DIM)) for a in (c, sm, sp))


def _layer(x, w_in, cmp_pos_k, cmp_w1_k, cmp_b1_k, cmp_w2_k, cmp_pos_v, cmp_w1_v, cmp_b1_v, cmp_w2_v,
           w_out, ln_gain, ln_bias, tables):
    batch, seq_len, _ = x.shape
    kv = D_NSA_KV
    sizes = (D_NSA, kv, kv, kv, kv, kv, kv, N_NSA_BRANCHES * NSA_HEADS, D_NSA, D_MOBA, D_MOBA, D_MOBA, D_MOBA)
    offs = np.concatenate([[0], np.cumsum(sizes)])
    col = lambda s: w_in[:, int(offs[s]):int(offs[s + 1])]
    zero_half = jnp.zeros((D_MODEL, HEAD_DIM), F32)
    q_blocks = []
    for h in range(NSA_HEADS):
        wq_h = col(0)[:, h * HEAD_DIM:(h + 1) * HEAD_DIM]
        q_blocks += [wq_h, zero_half] if h // NSA_GROUP == 0 else [zero_half, wq_h]
    w_qk = jnp.concatenate(q_blocks + [col(1), col(9), col(10)], axis=1)
    gate_pad = jnp.zeros((D_MODEL, LANES - sizes[7]), F32)
    w_smooth = jnp.concatenate([col(2), col(3), col(4), col(5), col(6), col(7), gate_pad,
                                col(8), col(11), col(12)], axis=1)

    x2 = x.reshape(batch * seq_len, D_MODEL)
    (qn, qr, kc_raw, mq, mk_ext, kmean), (vc_raw, ks_ext, vs, kw, vw, gl, z, mv, mz) = _project(
        x2, w_qk, w_smooth, tables, seq_len)
    kc = _compress(kc_raw, cmp_pos_k, cmp_w1_k, cmp_b1_k, cmp_w2_k, batch, seq_len)
    vc = _compress(vc_raw, cmp_pos_v, cmp_w1_v, cmp_b1_v, cmp_w2_v, batch, seq_len)
    y_nsa = _nsa_attention(qn, qr, kc, vc, ks_ext, vs, kw, vw, gl, z, batch, seq_len)
    y_moba = _moba_attention(mq, kmean, mk_ext, mv, mz, batch, seq_len)
    out = _out_project(x2, y_nsa.reshape(batch * seq_len, D_NSA), y_moba.reshape(batch * seq_len, D_MOBA),
                       w_out, ln_gain, ln_bias)
    return out.reshape(batch, seq_len, D_MODEL)


def kernel(x, w_in, cmp_pos_k, cmp_w1_k, cmp_b1_k, cmp_w2_k, cmp_pos_v, cmp_w1_v, cmp_b1_v, cmp_w2_v,
           w_out, ln_gain, ln_bias):
    tables = _rope_tables(x.shape[1])
    h = x
    for layer in range(w_in.shape[0]):
        h = _layer(h, w_in[layer], cmp_pos_k[layer], cmp_w1_k[layer], cmp_b1_k[layer], cmp_w2_k[layer],
                   cmp_pos_v[layer], cmp_w1_v[layer], cmp_b1_v[layer], cmp_w2_v[layer],
                   w_out[layer], ln_gain[layer], ln_bias[layer], tables)
    return h
```

```python
import functools

import numpy as np
import jax
import jax.numpy as jnp
from jax import lax
from jax.experimental import pallas as pl
from jax.experimental.pallas import tpu as pltpu

D_MODEL = 1024
HEAD_DIM = 64
NSA_HEADS = 8
NSA_KV_HEADS = 2
NSA_GROUP = NSA_HEADS // NSA_KV_HEADS
MOBA_HEADS = 8
D_NSA = NSA_HEADS * HEAD_DIM
D_NSA_KV = NSA_KV_HEADS * HEAD_DIM
D_MOBA = MOBA_HEADS * HEAD_DIM
N_NSA_BRANCHES = 3
ROPE_THETA = 500000.0
ROPE_DIM = HEAD_DIM // 4
CMP_BLOCK = 32
CMP_STRIDE = 16
CMP_HIDDEN = 4 * HEAD_DIM
SEL_BLOCK = 64
SEL_TOPK = 16
N_FORCED = 3
WINDOW = 512
MOBA_BLOCK = 256
MOBA_TOPK = 3
LN_EPS = 1e-5
DEPTH = 1
DEEPNORM_ALPHA = (2.0 * DEPTH) ** 0.25
NEG_BIG = -1e30
POS_BIG = 1e30
MASK_BIAS = -1e30
M_INIT = -1e29
LOG2E = 1.4426950408889634
Q_SCALE = HEAD_DIM ** -0.5 * LOG2E

LANES = 128
PROJ_TM = 512
ATT_TQ = 256
MOBA_TQ = 512
MOBA_PAIRS = 2
ATT_TK = 512
VMEM_LIMIT = 52 * 1024 * 1024

F32 = jnp.float32
BF16 = jnp.bfloat16
NT_DIMS = (((1,), (1,)), ((), ()))


def _dot(a, b):
    return jnp.dot(a, b, preferred_element_type=F32)


def _dot_nt(a, b):
    return lax.dot_general(a, b, NT_DIMS, preferred_element_type=F32)


def _rope(a, c, sm, sp):
    outs = []
    for p in range(a.shape[1] // LANES):
        s = a[:, p * LANES:(p + 1) * LANES]
        outs.append(s * c + pltpu.roll(s, LANES - ROPE_DIM // 2, 1) * sm + pltpu.roll(s, ROPE_DIM // 2, 1) * sp)
    return outs[0] if len(outs) == 1 else jnp.concatenate(outs, axis=1)


def _sigmoid(x):
    return 1.0 / (1.0 + jnp.exp(-x))


def _block_of(t, block):
    assert block & (block - 1) == 0
    return lax.shift_right_logical(t, jnp.int32(block.bit_length() - 1))


def _proj_qk_kernel(x_ref, w_ref, c_ref, sm_ref, sp_ref,
                    qn_ref, qr_ref, kc_ref, mq_ref, mk_ref, km_ref, wbf_ref, *, seq_len):
    i = pl.program_id(0)
    tm = x_ref.shape[0]

    @pl.when(i == 0)
    def _():
        wbf_ref[...] = w_ref[...].astype(BF16)

    acc = _dot(x_ref[...].astype(BF16), wbf_ref[...])
    c, sm, sp = c_ref[...], sm_ref[...], sp_ref[...]
    scale = Q_SCALE
    nq = NSA_HEADS * LANES

    q = acc[:, 0:nq] * scale
    qn_ref[...] = q.astype(BF16)
    qr_ref[...] = _rope(q, c, sm, sp).astype(BF16)
    kc_ref[...] = acc[:, nq:nq + D_NSA_KV]
    o = nq + D_NSA_KV
    mq_ref[...] = _rope(acc[:, o:o + D_MOBA] * scale, c, sm, sp).astype(BF16)
    mk = _rope(acc[:, o + D_MOBA:o + 2 * D_MOBA], c, sm, sp)

    t = (i * tm) % seq_len + lax.broadcasted_iota(jnp.int32, (tm, LANES), 0)
    lane = lax.broadcasted_iota(jnp.int32, (tm, LANES), 1)
    onehot = jnp.where(lane == _block_of(t, MOBA_BLOCK), 1.0, 0.0).astype(BF16)
    for p in range(D_MOBA // LANES):
        mk_ref[:, 2 * p * LANES:(2 * p + 1) * LANES] = mk[:, p * LANES:(p + 1) * LANES].astype(BF16)
        mk_ref[:, (2 * p + 1) * LANES:(2 * p + 2) * LANES] = onehot
    for r in range(tm // MOBA_BLOCK):
        blk = mk[r * MOBA_BLOCK:(r + 1) * MOBA_BLOCK]
        km_ref[r] = jnp.sum(blk, axis=0, keepdims=True) / MOBA_BLOCK


def _proj_smooth_kernel(x_ref, w_ref, c_ref, sm_ref, sp_ref,
                        vc_ref, ks_ref, vs_ref, kw_ref, vw_ref, g_ref, z_ref, mv_ref, mz_ref,
                        wbf_ref, *, seq_len):
    i = pl.program_id(0)
    tm = x_ref.shape[0]

    @pl.when(i == 0)
    def _():
        wbf_ref[...] = w_ref[...].astype(BF16)

    acc = _dot(x_ref[...].astype(BF16), wbf_ref[...])
    c, sm, sp = c_ref[...], sm_ref[...], sp_ref[...]
    kv = D_NSA_KV
    vc_ref[...] = acc[:, 0:kv]
    t = (i * tm) % seq_len + lax.broadcasted_iota(jnp.int32, (tm, LANES), 0)
    lane = lax.broadcasted_iota(jnp.int32, (tm, LANES), 1)
    ks_ref[:, 0:LANES] = _rope(acc[:, kv:2 * kv], c, sm, sp).astype(BF16)
    ks_ref[:, LANES:2 * LANES] = jnp.where(lane == _block_of(t, SEL_BLOCK), 1.0, 0.0).astype(BF16)
    ones = jnp.ones((tm, LANES), BF16)
    vs_ref[:, 0:LANES] = acc[:, 2 * kv:3 * kv].astype(BF16)
    vs_ref[:, LANES:2 * LANES] = ones
    kw_ref[...] = _rope(acc[:, 3 * kv:4 * kv], c, sm, sp).astype(BF16)
    vw_ref[...] = acc[:, 4 * kv:5 * kv].astype(BF16)
    g_ref[...] = acc[:, 5 * kv:6 * kv]
    o = 6 * kv
    z_ref[...] = acc[:, o:o + D_NSA]
    mv = acc[:, o + D_NSA:o + D_NSA + D_MOBA].astype(BF16)
    for p in range(D_MOBA // LANES):
        mv_ref[:, 2 * p * LANES:(2 * p + 1) * LANES] = mv[:, p * LANES:(p + 1) * LANES]
        mv_ref[:, (2 * p + 1) * LANES:(2 * p + 2) * LANES] = ones
    mz_ref[...] = acc[:, o + D_NSA + D_MOBA:o + D_NSA + 2 * D_MOBA]


def _row_spec(tm, width):
    return pl.BlockSpec((tm, width), lambda i: (i, 0))


def _table_spec(tm, seq_len):
    nt = seq_len // tm
    return pl.BlockSpec((tm, LANES), lambda i: (i % nt, 0))


def _project(x2, w_qk, w_smooth, tables, seq_len):
    m = x2.shape[0]
    tm = PROJ_TM
    grid = (m // tm,)
    n_qk, n_smooth = w_qk.shape[1], w_smooth.shape[1]
    nq = NSA_HEADS * LANES
    params = pltpu.CompilerParams(dimension_semantics=("arbitrary",), vmem_limit_bytes=VMEM_LIMIT)
    tab_specs = [_table_spec(tm, seq_len)] * 3

    qk = pl.pallas_call(
        functools.partial(_proj_qk_kernel, seq_len=seq_len),
        grid=grid,
        in_specs=[_row_spec(tm, D_MODEL), pl.BlockSpec((D_MODEL, n_qk), lambda i: (0, 0))] + tab_specs,
        out_specs=[_row_spec(tm, nq), _row_spec(tm, nq), _row_spec(tm, D_NSA_KV), _row_spec(tm, D_MOBA),
                   _row_spec(tm, 2 * D_MOBA),
                   pl.BlockSpec((tm // MOBA_BLOCK, 1, D_MOBA), lambda i: (i, 0, 0))],
        out_shape=[jax.ShapeDtypeStruct((m, nq), BF16), jax.ShapeDtypeStruct((m, nq), BF16),
                   jax.ShapeDtypeStruct((m, D_NSA_KV), F32), jax.ShapeDtypeStruct((m, D_MOBA), BF16),
                   jax.ShapeDtypeStruct((m, 2 * D_MOBA), BF16),
                   jax.ShapeDtypeStruct((m // MOBA_BLOCK, 1, D_MOBA), F32)],
        scratch_shapes=[pltpu.VMEM((D_MODEL, n_qk), BF16)],
        compiler_params=params,
        name="proj_qk",
    )(x2, w_qk, *tables)

    smooth = pl.pallas_call(
        functools.partial(_proj_smooth_kernel, seq_len=seq_len),
        grid=grid,
        in_specs=[_row_spec(tm, D_MODEL), pl.BlockSpec((D_MODEL, n_smooth), lambda i: (0, 0))] + tab_specs,
        out_specs=[_row_spec(tm, D_NSA_KV), _row_spec(tm, 2 * LANES), _row_spec(tm, 2 * LANES),
                   _row_spec(tm, D_NSA_KV), _row_spec(tm, D_NSA_KV), _row_spec(tm, LANES),
                   _row_spec(tm, D_NSA), _row_spec(tm, 2 * D_MOBA), _row_spec(tm, D_MOBA)],
        out_shape=[jax.ShapeDtypeStruct((m, D_NSA_KV), F32), jax.ShapeDtypeStruct((m, 2 * LANES), BF16),
                   jax.ShapeDtypeStruct((m, 2 * LANES), BF16), jax.ShapeDtypeStruct((m, D_NSA_KV), BF16),
                   jax.ShapeDtypeStruct((m, D_NSA_KV), BF16), jax.ShapeDtypeStruct((m, LANES), F32),
                   jax.ShapeDtypeStruct((m, D_NSA), F32), jax.ShapeDtypeStruct((m, 2 * D_MOBA), BF16),
                   jax.ShapeDtypeStruct((m, D_MOBA), F32)],
        scratch_shapes=[pltpu.VMEM((D_MODEL, n_smooth), BF16)],
        compiler_params=params,
        name="proj_smooth",
    )(x2, w_smooth, *tables)
    return qk, smooth


def _compress_kernel(ch_ref, pos_ref, w1_ref, b1_ref, w2_ref, out_ref):
    ch = ch_ref[0]
    n = ch.shape[0]
    parts = []
    for half in range(CMP_BLOCK // CMP_STRIDE):
        parts.append(_dot((ch + pos_ref[half]).astype(BF16), w1_ref[half].astype(BF16)))
    nxt = pltpu.roll(parts[1], n - 1, 0)
    row = lax.broadcasted_iota(jnp.int32, nxt.shape, 0)
    h = parts[0] + jnp.where(row < n - 1, nxt, 0.0) + b1_ref[...]
    h = h * _sigmoid(h)
    out_ref[0] = _dot(h.astype(BF16), w2_ref[...].astype(BF16))


def _compress(raw, pos, w1, b1, w2, batch, seq_len):
    g = NSA_KV_HEADS
    n_chunks = seq_len // CMP_STRIDE
    feat = CMP_STRIDE * D_NSA_KV
    ch = raw.reshape(batch, n_chunks, feat)
    n_half = CMP_BLOCK // CMP_STRIDE
    eye = jnp.eye(g, dtype=F32)
    w1r = w1.reshape(n_half, CMP_STRIDE, HEAD_DIM, CMP_HIDDEN)
    w1b = jnp.einsum('aldh,gk->algdkh', w1r, eye).reshape(n_half, feat, g * CMP_HIDDEN)
    posb = jnp.broadcast_to(pos.reshape(n_half, CMP_STRIDE, 1, HEAD_DIM),
                            (n_half, CMP_STRIDE, g, HEAD_DIM)).reshape(n_half, 1, feat)
    b1b = jnp.tile(b1.reshape(1, CMP_HIDDEN), (1, g))
    w2b = jnp.einsum('hd,gk->ghkd', w2, eye).reshape(g * CMP_HIDDEN, g * HEAD_DIM)
    return pl.pallas_call(
        _compress_kernel,
        grid=(batch,),
        in_specs=[pl.BlockSpec((1, n_chunks, feat), lambda b: (b, 0, 0)),
                  pl.BlockSpec((n_half, 1, feat), lambda b: (0, 0, 0)),
                  pl.BlockSpec((n_half, feat, g * CMP_HIDDEN), lambda b: (0, 0, 0)),
                  pl.BlockSpec((1, g * CMP_HIDDEN), lambda b: (0, 0)),
                  pl.BlockSpec((g * CMP_HIDDEN, g * HEAD_DIM), lambda b: (0, 0))],
        out_specs=pl.BlockSpec((1, n_chunks, g * HEAD_DIM), lambda b: (b, 0, 0)),
        out_shape=jax.ShapeDtypeStruct((batch, n_chunks, g * HEAD_DIM), F32),
        compiler_params=pltpu.CompilerParams(dimension_semantics=("arbitrary",), vmem_limit_bytes=VMEM_LIMIT),
        name="compress",
    )(ch, posb, w1b, b1b, w2b)


def _top_k_rows(val, k, sel=None):
    row = lax.broadcasted_iota(jnp.int32, val.shape, 0).astype(F32)
    sel = jnp.zeros(val.shape, F32) if sel is None else sel
    for _ in range(k):
        m = jnp.max(val, axis=0, keepdims=True)
        idx = jnp.min(jnp.where(val == m, row, float(LANES)), axis=0, keepdims=True)
        hit = row == idx
        sel = jnp.where(hit, 1.0, sel)
        val = jnp.where(hit, -jnp.inf, val)
    return sel


def _exp2_rows(s, v_ext):
    n = s.shape[1] // LANES
    part = s[:, 0:LANES]
    for c in range(1, n):
        part = jnp.maximum(part, s[:, c * LANES:(c + 1) * LANES])
    m = jnp.maximum(jnp.max(part, axis=1, keepdims=True), M_INIT)
    e = jnp.exp2(s - m)
    ov = _dot(e.astype(BF16), v_ext)
    d = ov[:, LANES:2 * LANES]
    return e, ov[:, 0:LANES], 1.0 / jnp.where(d > 0, d, 1.0)


def _online_step(s, v_ext, m_ref, acc_ref):
    tk = s.shape[1]
    part = s[:, 0:LANES]
    for c in range(1, tk // LANES):
        part = jnp.maximum(part, s[:, c * LANES:(c + 1) * LANES])
    m_prev = m_ref[...]
    m_new = jnp.maximum(m_prev, jnp.max(part, axis=1, keepdims=True))
    alpha = jnp.exp2(m_prev - m_new)
    p = jnp.exp2(s - jnp.tile(m_new, (1, tk // LANES)))
    acc_ref[...] = jnp.tile(alpha, (1, 2)) * acc_ref[...] + _dot(p.astype(BF16), v_ext)
    m_ref[...] = m_new


def _dense_masked_attention(streams, t_all, n_chunks):
    tk = ATT_TK

    def scores(st, j):
        qs, k_ref, _, col = st[0:4]
        return _dot_nt(qs, k_ref[0, pl.ds(pl.multiple_of(j * tk, tk), tk), col:col + 2 * LANES])

    def step(st, s, j):
        _, _, v_ref, col, m_ref, acc_ref = st[0:6]
        _online_step(s, v_ref[0, pl.ds(pl.multiple_of(j * tk, tk), tk), col:col + 2 * LANES], m_ref, acc_ref)

    def causal(s, j):
        kpos = j * tk + lax.broadcasted_iota(jnp.int32, (1, tk), 1)
        return jnp.where(kpos <= t_all, s, MASK_BIAS)

    last = n_chunks - 1
    for st in streams:
        m_ref, acc_ref, sa_ref = st[4], st[5], st[6]
        m_ref[...] = jnp.full(m_ref.shape, M_INIT, F32)
        acc_ref[...] = jnp.zeros(acc_ref.shape, F32)
        sa_ref[...] = scores(st, 0)

    def body(jj, carry):
        j = 2 * jj
        for st in streams:
            st[7][...] = scores(st, j + 1)
            step(st, st[6][...], j)
        for st in streams:
            st[6][...] = scores(st, j + 2)
            step(st, st[7][...], j + 1)
        return carry

    lax.fori_loop(0, last // 2, body, 0)

    @pl.when(last % 2 == 1)
    def _():
        for st in streams:
            st[7][...] = scores(st, last)
            step(st, st[6][...], last - 1)
        for st in streams:
            step(st, causal(st[7][...], last), last)

    @pl.when(last % 2 == 0)
    def _():
        for st in streams:
            step(st, causal(st[6][...], last), last)

    outs = []
    for st in streams:
        acc = st[5][...]
        outs.append(acc[:, 0:LANES] / acc[:, LANES:2 * LANES])
    return outs


def _nsa_kernel(qn_ref, qr_ref, kc_ref, vc_ref, ks_ref, vs_ref, kw_ref, vw_ref, g_ref, z_ref,
                ovt_ref, eg_ref, y_ref, m_ref, acc_ref, sa_ref, sb_ref):
    i = pl.program_id(1)
    tq, tk = ATT_TQ, ATT_TK
    r = NSA_GROUP
    s0 = i * tq
    t = s0 + lax.broadcasted_iota(jnp.int32, (tq, 1), 0)
    t_all = jnp.concatenate([t] * r, axis=0)
    lane = lax.broadcasted_iota(jnp.int32, (tq, LANES), 1)

    def per_head_bias(s, bias):
        return jnp.concatenate([s[j * tq:(j + 1) * tq] + bias for j in range(r)], axis=0)

    kc_bf = kc_ref[0].astype(BF16)
    n_cmp = kc_bf.shape[0]
    vc_ext = jnp.concatenate([vc_ref[0].astype(BF16), jnp.ones((n_cmp, LANES), BF16)], axis=1)
    cmp_end = lax.broadcasted_iota(jnp.int32, (1, n_cmp), 1) * CMP_STRIDE + (CMP_BLOCK - 1)
    cmp_bias = jnp.where(cmp_end <= t, 0.0, MASK_BIAS)

    win_start = jnp.maximum(s0 - WINDOW, 0)
    n_win = WINDOW + tq
    kwin = kw_ref[0, pl.ds(pl.multiple_of(win_start, tq), n_win), :]
    vwin_ext = jnp.concatenate([vw_ref[0, pl.ds(pl.multiple_of(win_start, tq), n_win), :],
                                jnp.ones((n_win, LANES), BF16)], axis=1)
    kpos_w = win_start + lax.broadcasted_iota(jnp.int32, (1, n_win), 1)
    win_bias = jnp.where((kpos_w <= t) & (kpos_w > t - WINDOW), 0.0, MASK_BIAS)

    blk = lax.broadcasted_iota(jnp.int32, (LANES, tq), 0)
    qblk = _block_of(s0 + lax.broadcasted_iota(jnp.int32, (1, tq), 1), SEL_BLOCK)
    forced = (blk == 0) | (blk == qblk) | (blk == qblk - 1)
    forced_sel = jnp.where(forced, 1.0, 0.0)
    ovt = ovt_ref[...]

    o_cmp, o_win, qs_all = [], [], []
    for g in range(NSA_KV_HEADS):
        heads = [g * r + j for j in range(r)]
        qn = jnp.concatenate([qn_ref[0, :, h * LANES:(h + 1) * LANES] for h in heads], axis=0)
        qr = jnp.concatenate([qr_ref[0, :, h * LANES:(h + 1) * LANES] for h in heads], axis=0)

        e, ov, rinv = _exp2_rows(per_head_bias(_dot_nt(qn, kc_bf), cmp_bias), vc_ext)
        pc = e * jnp.tile(rinv, (1, n_cmp // LANES))
        o_cmp.append(ov * rinv)

        pc_sum = pc[0:tq]
        for j in range(1, r):
            pc_sum = pc_sum + pc[j * tq:(j + 1) * tq]
        imp = _dot_nt(ovt, pc_sum.astype(BF16))
        imp = jnp.where(forced, -jnp.inf, jnp.where(blk > qblk, NEG_BIG, imp))
        sel = _top_k_rows(imp, SEL_TOPK - N_FORCED, forced_sel)
        sel_bias = jnp.where(sel.T > 0.5, 0.0, MASK_BIAS).astype(BF16)
        qs_all.append(jnp.concatenate([qr, jnp.concatenate([sel_bias] * r, axis=0)], axis=1))

        _, ov, rinv = _exp2_rows(per_head_bias(_dot_nt(qr, kwin), win_bias), vwin_ext)
        o_win.append(ov * rinv)

    rows = r * tq
    stream = (jnp.concatenate(qs_all, axis=0), ks_ref, vs_ref, 0, m_ref, acc_ref, sa_ref, sb_ref)
    t_rows = jnp.concatenate([t_all] * NSA_KV_HEADS, axis=0)
    o_both, = _dense_masked_attention([stream], t_rows, (s0 + tq + tk - 1) // tk)
    o_slc = [o_both[g * rows:(g + 1) * rows] for g in range(NSA_KV_HEADS)]

    gates = _dot(_sigmoid(g_ref[0]).astype(BF16), eg_ref[...])
    z = z_ref[0]
    for p in range(D_NSA // LANES):
        g = (2 * p) // r
        y = None
        for b, o_all in enumerate((o_cmp, o_slc, o_win)):
            o = o_all[g]
            j0, j1 = (2 * p) % r, (2 * p + 1) % r
            a0 = o[j0 * tq:(j0 + 1) * tq]
            a1 = o[j1 * tq:(j1 + 1) * tq]
            if g != 0:
                a0 = pltpu.roll(a0, HEAD_DIM, 1)
            if g != 1:
                a1 = pltpu.roll(a1, HEAD_DIM, 1)
            pair = jnp.where(lane < HEAD_DIM, a0, a1)
            term = gates[:, b * D_NSA + p * LANES:b * D_NSA + (p + 1) * LANES] * pair
            y = term if y is None else y + term
        zp = z[:, p * LANES:(p + 1) * LANES]
        y_ref[0, :, p * LANES:(p + 1) * LANES] = (y * (zp * _sigmoid(zp))).astype(BF16)


def _nsa_attention(qn, qr, kc, vc, ks_ext, vs, kw, vw, gl, z, batch, seq_len):
    tq = ATT_TQ
    n_cmp = seq_len // CMP_STRIDE
    n_sel = seq_len // SEL_BLOCK
    c_start = np.arange(n_cmp)[:, None] * CMP_STRIDE
    s_start = np.arange(LANES)[None, :] * SEL_BLOCK
    overlap = ((c_start < s_start + SEL_BLOCK) & (c_start + CMP_BLOCK > s_start) & (np.arange(LANES)[None, :] < n_sel))
    overlap_t = jnp.asarray(overlap.T.astype(np.float32), dtype=BF16)
    eg = np.zeros((LANES, N_NSA_BRANCHES * D_NSA), np.float32)
    for h in range(NSA_HEADS):
        for b in range(N_NSA_BRANCHES):
            eg[h * N_NSA_BRANCHES + b, b * D_NSA + h * HEAD_DIM:b * D_NSA + (h + 1) * HEAD_DIM] = 1.0
    eg = jnp.asarray(eg, dtype=BF16)

    def r3(a):
        return a.reshape(batch, seq_len, a.shape[-1])

    tile = lambda w: pl.BlockSpec((1, tq, w), lambda b, i: (b, i, 0))
    full = lambda n, w: pl.BlockSpec((1, n, w), lambda b, i: (b, 0, 0), pipeline_mode=pl.Buffered(1))
    const = lambda shape: pl.BlockSpec(shape, lambda b, i: (0, 0))
    rows = NSA_HEADS * tq
    return pl.pallas_call(
        _nsa_kernel,
        grid=(batch, seq_len // tq),
        in_specs=[tile(NSA_HEADS * LANES), tile(NSA_HEADS * LANES), full(n_cmp, D_NSA_KV), full(n_cmp, D_NSA_KV),
                  full(seq_len, 2 * LANES), full(seq_len, 2 * LANES), full(seq_len, D_NSA_KV),
                  full(seq_len, D_NSA_KV), tile(LANES), tile(D_NSA),
                  const(overlap_t.shape), const(eg.shape)],
        out_specs=tile(D_NSA),
        out_shape=jax.ShapeDtypeStruct((batch, seq_len, D_NSA), BF16),
        scratch_shapes=[pltpu.VMEM((rows, LANES), F32), pltpu.VMEM((rows, 2 * LANES), F32),
                        pltpu.VMEM((rows, ATT_TK), F32), pltpu.VMEM((rows, ATT_TK), F32)],
        compiler_params=pltpu.CompilerParams(dimension_semantics=("arbitrary", "arbitrary"),
                                             vmem_limit_bytes=VMEM_LIMIT),
        name="nsa_attention",
    )(r3(qn), r3(qr), kc, vc, r3(ks_ext), r3(vs), r3(kw), r3(vw), r3(gl), r3(z), overlap_t, eg)


def _moba_kernel(q_ref, km_ref, mk_ref, mv_ref, z_ref, y_ref, *scratch):
    i = pl.program_id(2)
    tq, tk = MOBA_TQ, ATT_TK
    s0 = i * tq
    t = s0 + lax.broadcasted_iota(jnp.int32, (tq, 1), 0)
    t_all = jnp.concatenate([t, t], axis=0)
    lane = lax.broadcasted_iota(jnp.int32, (tq, LANES), 1)
    blk = lax.broadcasted_iota(jnp.int32, (LANES, tq), 0)
    own = _block_of(s0 + lax.broadcasted_iota(jnp.int32, (1, tq), 1), MOBA_BLOCK)
    nb = km_ref.shape[1]

    streams = []
    for p in range(MOBA_PAIRS):
        q = q_ref[0, :, p * LANES:(p + 1) * LANES]
        km = km_ref[0, :, p * LANES:(p + 1) * LANES]
        if nb < LANES:
            km = jnp.concatenate([km, jnp.zeros((LANES - nb, LANES), F32)], axis=0)
        km_bf = km.astype(BF16)
        qs = []
        for hh in range(2):
            qh = jnp.where((lane >= hh * HEAD_DIM) & (lane < (hh + 1) * HEAD_DIM), q, jnp.zeros_like(q))
            sg = _dot_nt(km_bf, qh)
            chosen = _top_k_rows(jnp.where(blk < own, sg, NEG_BIG), MOBA_TOPK)
            visible = jnp.where(blk < own, chosen, jnp.where(blk == own, 1.0, 0.0))
            bias = jnp.where(visible.T > 0.5, 0.0, MASK_BIAS).astype(BF16)
            qs.append(jnp.concatenate([qh, bias], axis=1))
        streams.append((jnp.concatenate(qs, axis=0), mk_ref, mv_ref, p * 2 * LANES) + tuple(scratch[4 * p:4 * p + 4]))

    outs = _dense_masked_attention(streams, t_all, (s0 + tq + tk - 1) // tk)
    for p, o in enumerate(outs):
        y = jnp.where(lane < HEAD_DIM, o[0:tq], o[tq:2 * tq])
        z = z_ref[0, :, p * LANES:(p + 1) * LANES]
        y_ref[0, :, p * LANES:(p + 1) * LANES] = (y * (z * _sigmoid(z))).astype(BF16)


def _moba_attention(mq, kmean, mk_ext, mv, mz, batch, seq_len):
    tq = MOBA_TQ
    nb = seq_len // MOBA_BLOCK
    w = MOBA_PAIRS * LANES
    n_steps = D_MOBA // w

    def r3(a):
        return a.reshape(batch, seq_len, a.shape[-1])

    tile = pl.BlockSpec((1, tq, w), lambda b, p, i: (b, i, p))
    full = lambda n, width: pl.BlockSpec((1, n, width), lambda b, p, i: (b, 0, p), pipeline_mode=pl.Buffered(1))
    rows = 2 * tq
    per_pair = [pltpu.VMEM((rows, LANES), F32), pltpu.VMEM((rows, 2 * LANES), F32),
                pltpu.VMEM((rows, ATT_TK), F32), pltpu.VMEM((rows, ATT_TK), F32)]
    return pl.pallas_call(
        _moba_kernel,
        grid=(batch, n_steps, seq_len // tq),
        in_specs=[tile, full(nb, w), full(seq_len, 2 * w), full(seq_len, 2 * w), tile],
        out_specs=tile,
        out_shape=jax.ShapeDtypeStruct((batch, seq_len, D_MOBA), BF16),
        scratch_shapes=per_pair * MOBA_PAIRS,
        compiler_params=pltpu.CompilerParams(dimension_semantics=("arbitrary", "arbitrary", "arbitrary"),
                                             vmem_limit_bytes=VMEM_LIMIT),
        name="moba_attention",
    )(r3(mq), kmean.reshape(batch, nb, D_MOBA), r3(mk_ext), r3(mv), r3(mz))


def _out_kernel(x_ref, yn_ref, ym_ref, w_ref, gain_ref, bias_ref, o_ref, wbf_ref):
    @pl.when(pl.program_id(0) == 0)
    def _():
        wbf_ref[...] = w_ref[...].astype(BF16)

    sub = _dot(yn_ref[...], wbf_ref[0:D_NSA, :]) + _dot(ym_ref[...], wbf_ref[D_NSA:D_NSA + D_MOBA, :])
    h = DEEPNORM_ALPHA * x_ref[...] + sub
    mu = jnp.mean(h, axis=1, keepdims=True)
    d = h - mu
    var = jnp.mean(d * d, axis=1, keepdims=True)
    o_ref[...] = d * lax.rsqrt(var + LN_EPS) * gain_ref[...] + bias_ref[...]


def _out_project(x2, y_nsa, y_moba, w_out, gain, bias):
    m = x2.shape[0]
    tm = PROJ_TM
    d_mix = D_NSA + D_MOBA
    return pl.pallas_call(
        _out_kernel,
        grid=(m // tm,),
        in_specs=[_row_spec(tm, D_MODEL), _row_spec(tm, D_NSA), _row_spec(tm, D_MOBA),
                  pl.BlockSpec((d_mix, D_MODEL), lambda i: (0, 0)),
                  pl.BlockSpec((1, D_MODEL), lambda i: (0, 0)), pl.BlockSpec((1, D_MODEL), lambda i: (0, 0))],
        out_specs=_row_spec(tm, D_MODEL),
        out_shape=jax.ShapeDtypeStruct((m, D_MODEL), F32),
        scratch_shapes=[pltpu.VMEM((d_mix, D_MODEL), BF16)],
        compiler_params=pltpu.CompilerParams(dimension_semantics=("arbitrary",), vmem_limit_bytes=VMEM_LIMIT),
        name="out_proj_norm",
    )(x2, y_nsa, y_moba, w_out, gain.reshape(1, D_MODEL), bias.reshape(1, D_MODEL))


def _rope_tables(seq_len):
    half = ROPE_DIM // 2
    inv_freq = ROPE_THETA ** (-jnp.arange(0, ROPE_DIM, 2, dtype=F32) / ROPE_DIM)
    ang = jnp.arange(seq_len, dtype=F32)[:, None] * inv_freq[None, :]
    cos, sin = jnp.cos(ang), jnp.sin(ang)
    zeros = jnp.zeros((seq_len, HEAD_DIM - ROPE_DIM), F32)
    zh = jnp.zeros((seq_len, half), F32)
    c = jnp.concatenate([cos, cos, zeros + 1.0], axis=1)
    sm = jnp.concatenate([-sin, zh, zeros], axis=1)
    sp = jnp.concatenate([zh, sin, zeros], axis=1)
    return tuple(jnp.tile(a, (1, LANES // HEAD_DIM)) for a in (c, sm, sp))


def _layer(x, w_in, cmp_pos_k, cmp_w1_k, cmp_b1_k, cmp_w2_k, cmp_pos_v, cmp_w1_v, cmp_b1_v, cmp_w2_v,
           w_out, ln_gain, ln_bias, tables):
    batch, seq_len, _ = x.shape
    kv = D_NSA_KV
    sizes = (D_NSA, kv, kv, kv, kv, kv, kv, N_NSA_BRANCHES * NSA_HEADS, D_NSA, D_MOBA, D_MOBA, D_MOBA, D_MOBA)
    offs = np.concatenate([[0], np.cumsum(sizes)])
    col = lambda s: w_in[:, int(offs[s]):int(offs[s + 1])]
    zero_half = jnp.zeros((D_MODEL, HEAD_DIM), F32)
    q_blocks = []
    for h in range(NSA_HEADS):
        wq_h = col(0)[:, h * HEAD_DIM:(h + 1) * HEAD_DIM]
        q_blocks += [wq_h, zero_half] if h // NSA_GROUP == 0 else [zero_half, wq_h]
    w_qk = jnp.concatenate(q_blocks + [col(1), col(9), col(10)], axis=1)
    gate_pad = jnp.zeros((D_MODEL, LANES - sizes[7]), F32)
    w_smooth = jnp.concatenate([col(2), col(3), col(4), col(5), col(6), col(7), gate_pad,
                                col(8), col(11), col(12)], axis=1)

    x2 = x.reshape(batch * seq_len, D_MODEL)
    (qn, qr, kc_raw, mq, mk_ext, kmean), (vc_raw, ks_ext, vs, kw, vw, gl, z, mv, mz) = _project(
        x2, w_qk, w_smooth, tables, seq_len)
    kc = _compress(kc_raw, cmp_pos_k, cmp_w1_k, cmp_b1_k, cmp_w2_k, batch, seq_len)
    vc = _compress(vc_raw, cmp_pos_v, cmp_w1_v, cmp_b1_v, cmp_w2_v, batch, seq_len)
    y_nsa = _nsa_attention(qn, qr, kc, vc, ks_ext, vs, kw, vw, gl, z, batch, seq_len)
    y_moba = _moba_attention(mq, kmean, mk_ext, mv, mz, batch, seq_len)
    out = _out_project(x2, y_nsa.reshape(batch * seq_len, D_NSA), y_moba.reshape(batch * seq_len, D_MOBA),
                       w_out, ln_gain, ln_bias)
    return out.reshape(batch, seq_len, D_MODEL)


def kernel(x, w_in, cmp_pos_k, cmp_w1_k, cmp_b1_k, cmp_w2_k, cmp_pos_v, cmp_w1_v, cmp_b1_v, cmp_w2_v,
           w_out, ln_gain, ln_bias):
    tables = _rope_tables(x.shape[1])
    h = x
    for layer in range(w_in.shape[0]):
        h = _layer(h, w_in[layer], cmp_pos_k[layer], cmp_w1_k[layer], cmp_b1_k[layer], cmp_w2_k[layer],
                   cmp_pos_v[layer], cmp_w1_v[layer], cmp_b1_v[layer], cmp_w2_v[layer],
                   w_out[layer], ln_gain[layer], ln_bias[layer], tables)
    return h
```

```python
import numpy as np
import jax
import jax.numpy as jnp
from jax import lax
from jax.experimental import pallas as pl
from jax.experimental.pallas import tpu as pltpu

D_MODEL = 1024
HEAD_DIM = 64
NSA_HEADS = 8
NSA_KV_HEADS = 2
NSA_GROUP = NSA_HEADS // NSA_KV_HEADS
MOBA_HEADS = 8
D_NSA = NSA_HEADS * HEAD_DIM
D_NSA_KV = NSA_KV_HEADS * HEAD_DIM
D_MOBA = MOBA_HEADS * HEAD_DIM
N_NSA_BRANCHES = 3
ROPE_THETA = 500000.0
ROPE_DIM = HEAD_DIM // 4
CMP_BLOCK = 32
CMP_STRIDE = 16
CMP_HIDDEN = 4 * HEAD_DIM
SEL_BLOCK = 64
SEL_TOPK = 16
N_FORCED = 3
WINDOW = 512
MOBA_BLOCK = 256
MOBA_TOPK = 3
LN_EPS = 1e-5
DEPTH = 1
DEEPNORM_ALPHA = (2.0 * DEPTH) ** 0.25
NEG_BIG = -1e30
POS_BIG = 1e30
MASK_BIAS = -1e30
M_INIT = -1e29
LOG2E = 1.4426950408889634
Q_SCALE = HEAD_DIM ** -0.5 * LOG2E

LANES = 128
PROJ_TM = 512
ATT_TQ = 256
MOBA_TQ = 512
MOBA_PAIRS = 2
ATT_TK = 512
VMEM_LIMIT = 52 * 1024 * 1024

F32 = jnp.float32
BF16 = jnp.bfloat16
NT_DIMS = (((1,), (1,)), ((), ()))


def _dot(a, b):
    return jnp.dot(a, b, preferred_element_type=F32)


def _dot_nt(a, b):
    return lax.dot_general(a, b, NT_DIMS, preferred_element_type=F32)


def _rope(a, c, sm, sp):
    outs = []
    for p in range(a.shape[1] // LANES):
        s = a[:, p * LANES:(p + 1) * LANES]
        outs.append(s * c + pltpu.roll(s, LANES - ROPE_DIM // 2, 1) * sm + pltpu.roll(s, ROPE_DIM // 2, 1) * sp)
    return outs[0] if len(outs) == 1 else jnp.concatenate(outs, axis=1)


def _sigmoid(x):
    return 1.0 / (1.0 + jnp.exp(-x))


def _block_of(t, block):
    assert block & (block - 1) == 0
    return lax.shift_right_logical(t, jnp.int32(block.bit_length() - 1))


def _proj_kernel(x_ref, w_ref, c_ref, sm_ref, sp_ref,
                 qn_ref, qr_ref, kc_ref, mq_ref, mk_ref, km_ref,
                 vc_ref, ks_ref, vs_ref, kw_ref, vw_ref, g_ref, z_ref, mv_ref, mz_ref):
    tm = x_ref.shape[0]
    xb = x_ref[...].astype(BF16)
    c, sm, sp = c_ref[...], sm_ref[...], sp_ref[...]
    nq = NSA_HEADS * LANES
    kv = D_NSA_KV
    n_first = nq + kv + 2 * D_MOBA

    acc = _dot(xb, w_ref[:, 0:n_first])
    q = acc[:, 0:nq] * Q_SCALE
    qn_ref[...] = q.astype(BF16)
    qr_ref[...] = _rope(q, c, sm, sp).astype(BF16)
    kc_ref[...] = acc[:, nq:nq + kv]
    o = nq + kv
    mq_ref[...] = _rope(acc[:, o:o + D_MOBA] * Q_SCALE, c, sm, sp).astype(BF16)
    mk = _rope(acc[:, o + D_MOBA:o + 2 * D_MOBA], c, sm, sp)
    mk_ref[...] = mk.astype(BF16)
    for r in range(tm // MOBA_BLOCK):
        blk = mk[r * MOBA_BLOCK:(r + 1) * MOBA_BLOCK]
        km_ref[r] = jnp.sum(blk, axis=0, keepdims=True) / MOBA_BLOCK

    acc = _dot(xb, w_ref[:, n_first:])
    vc_ref[...] = acc[:, 0:kv]
    ks_ref[...] = _rope(acc[:, kv:2 * kv], c, sm, sp).astype(BF16)
    vs_ref[...] = acc[:, 2 * kv:3 * kv].astype(BF16)
    kw_ref[...] = _rope(acc[:, 3 * kv:4 * kv], c, sm, sp).astype(BF16)
    vw_ref[...] = acc[:, 4 * kv:5 * kv].astype(BF16)
    g_ref[...] = acc[:, 5 * kv:6 * kv]
    o = 6 * kv
    z_ref[...] = acc[:, o:o + D_NSA].astype(BF16)
    mv_ref[...] = acc[:, o + D_NSA:o + D_NSA + D_MOBA].astype(BF16)
    mz_ref[...] = acc[:, o + D_NSA + D_MOBA:o + D_NSA + 2 * D_MOBA].astype(BF16)


def _row_spec(tm, width):
    return pl.BlockSpec((tm, width), lambda i: (i, 0))


def _project(x2, w_all, tables, seq_len):
    m = x2.shape[0]
    tm = PROJ_TM
    nt = seq_len // tm
    nq = NSA_HEADS * LANES
    table_spec = pl.BlockSpec((tm, LANES), lambda i: (i % nt, 0))
    widths = [(nq, BF16), (nq, BF16), (D_NSA_KV, F32), (D_MOBA, BF16), (D_MOBA, BF16), None,
              (D_NSA_KV, F32), (D_NSA_KV, BF16), (D_NSA_KV, BF16), (D_NSA_KV, BF16), (D_NSA_KV, BF16),
              (LANES, F32), (D_NSA, BF16), (D_MOBA, BF16), (D_MOBA, BF16)]
    out_specs, out_shape = [], []
    for wd in widths:
        if wd is None:
            out_specs.append(pl.BlockSpec((tm // MOBA_BLOCK, 1, D_MOBA), lambda i: (i, 0, 0)))
            out_shape.append(jax.ShapeDtypeStruct((m // MOBA_BLOCK, 1, D_MOBA), F32))
        else:
            out_specs.append(_row_spec(tm, wd[0]))
            out_shape.append(jax.ShapeDtypeStruct((m, wd[0]), wd[1]))
    return pl.pallas_call(
        _proj_kernel,
        grid=(m // tm,),
        in_specs=[_row_spec(tm, D_MODEL),
                  pl.BlockSpec(w_all.shape, lambda i: (0, 0), pipeline_mode=pl.Buffered(1))] + [table_spec] * 3,
        out_specs=out_specs,
        out_shape=out_shape,
        compiler_params=pltpu.CompilerParams(dimension_semantics=("arbitrary",), vmem_limit_bytes=VMEM_LIMIT),
        name="proj",
    )(x2, w_all, *tables)


def _compress_kernel(ch_ref, pos_ref, w1_ref, b1_ref, w2_ref, out_ref):
    ch = ch_ref[0]
    n = ch.shape[0]
    parts = []
    for half in range(CMP_BLOCK // CMP_STRIDE):
        parts.append(_dot((ch + pos_ref[half]).astype(BF16), w1_ref[half].astype(BF16)))
    nxt = pltpu.roll(parts[1], n - 1, 0)
    row = lax.broadcasted_iota(jnp.int32, nxt.shape, 0)
    h = parts[0] + jnp.where(row < n - 1, nxt, 0.0) + b1_ref[...]
    h = h * _sigmoid(h)
    out_ref[0] = _dot(h.astype(BF16), w2_ref[...].astype(BF16))


def _compress(raw, pos, w1, b1, w2, batch, seq_len):
    g = NSA_KV_HEADS
    n_chunks = seq_len // CMP_STRIDE
    feat = CMP_STRIDE * D_NSA_KV
    ch = raw.reshape(batch, n_chunks, feat)
    n_half = CMP_BLOCK // CMP_STRIDE
    eye = jnp.eye(g, dtype=F32)
    w1r = w1.reshape(n_half, CMP_STRIDE, HEAD_DIM, CMP_HIDDEN)
    w1b = jnp.einsum('aldh,gk->algdkh', w1r, eye).reshape(n_half, feat, g * CMP_HIDDEN)
    posb = jnp.broadcast_to(pos.reshape(n_half, CMP_STRIDE, 1, HEAD_DIM),
                            (n_half, CMP_STRIDE, g, HEAD_DIM)).reshape(n_half, 1, feat)
    b1b = jnp.tile(b1.reshape(1, CMP_HIDDEN), (1, g))
    w2b = jnp.einsum('hd,gk->ghkd', w2, eye).reshape(g * CMP_HIDDEN, g * HEAD_DIM)
    return pl.pallas_call(
        _compress_kernel,
        grid=(batch,),
        in_specs=[pl.BlockSpec((1, n_chunks, feat), lambda b: (b, 0, 0)),
                  pl.BlockSpec((n_half, 1, feat), lambda b: (0, 0, 0)),
                  pl.BlockSpec((n_half, feat, g * CMP_HIDDEN), lambda b: (0, 0, 0)),
                  pl.BlockSpec((1, g * CMP_HIDDEN), lambda b: (0, 0)),
                  pl.BlockSpec((g * CMP_HIDDEN, g * HEAD_DIM), lambda b: (0, 0))],
        out_specs=pl.BlockSpec((1, n_chunks, g * HEAD_DIM), lambda b: (b, 0, 0)),
        out_shape=jax.ShapeDtypeStruct((batch, n_chunks, g * HEAD_DIM), F32),
        compiler_params=pltpu.CompilerParams(dimension_semantics=("arbitrary",), vmem_limit_bytes=VMEM_LIMIT),
        name="compress",
    )(ch, posb, w1b, b1b, w2b)


def _top_k_rows(val, k, sel=None):
    row = lax.broadcasted_iota(jnp.int32, val.shape, 0).astype(F32)
    sel = jnp.zeros(val.shape, F32) if sel is None else sel
    for _ in range(k):
        m = jnp.max(val, axis=0, keepdims=True)
        idx = jnp.min(jnp.where(val == m, row, float(LANES)), axis=0, keepdims=True)
        hit = row == idx
        sel = jnp.where(hit, 1.0, sel)
        val = jnp.where(hit, -jnp.inf, val)
    return sel


def _exp2_rows(s, v_ext):
    n = s.shape[1] // LANES
    part = s[:, 0:LANES]
    for c in range(1, n):
        part = jnp.maximum(part, s[:, c * LANES:(c + 1) * LANES])
    m = jnp.maximum(jnp.max(part, axis=1, keepdims=True), M_INIT)
    e = jnp.exp2(s - m)
    ov = _dot(e.astype(BF16), v_ext)
    d = ov[:, LANES:2 * LANES]
    return e, ov[:, 0:LANES], 1.0 / jnp.where(d > 0, d, 1.0)


def _online_step(s, v_ext, m_ref, acc_ref):
    tk = s.shape[1]
    part = s[:, 0:LANES]
    for c in range(1, tk // LANES):
        part = jnp.maximum(part, s[:, c * LANES:(c + 1) * LANES])
    m_prev = m_ref[...]
    m_new = jnp.maximum(m_prev, jnp.max(part, axis=1, keepdims=True))
    alpha = jnp.exp2(m_prev - m_new)
    p = jnp.exp2(s - jnp.tile(m_new, (1, tk // LANES)))
    acc_ref[...] = jnp.tile(alpha, (1, 2)) * acc_ref[...] + _dot(p.astype(BF16), v_ext)
    m_ref[...] = m_new


def _dense_masked_attention(streams, t_all, n_chunks, block_size):
    tk = ATT_TK

    key_lane = lax.broadcasted_iota(jnp.int32, (tk, LANES), 1)
    key_row = lax.broadcasted_iota(jnp.int32, (tk, LANES), 0)
    ones = jnp.ones((tk, LANES), BF16)

    def scores(st, j):
        qs, k_ref, _, col = st[0:4]
        k = k_ref[0, pl.ds(pl.multiple_of(j * tk, tk), tk), col:col + LANES]
        onehot = jnp.where(key_lane == _block_of(j * tk + key_row, block_size), 1.0, 0.0).astype(BF16)
        return _dot_nt(qs, jnp.concatenate([k, onehot], axis=1))

    def step(st, s, j):
        _, _, v_ref, col, m_ref, acc_ref = st[0:6]
        v = v_ref[0, pl.ds(pl.multiple_of(j * tk, tk), tk), col:col + LANES]
        _online_step(s, jnp.concatenate([v, ones], axis=1), m_ref, acc_ref)

    def causal(s, j):
        kpos = j * tk + lax.broadcasted_iota(jnp.int32, (1, tk), 1)
        return jnp.where(kpos <= t_all, s, MASK_BIAS)

    last = n_chunks - 1
    for st in streams:
        m_ref, acc_ref, sa_ref = st[4], st[5], st[6]
        m_ref[...] = jnp.full(m_ref.shape, M_INIT, F32)
        acc_ref[...] = jnp.zeros(acc_ref.shape, F32)
        sa_ref[...] = scores(st, 0)

    def pair(j):
        for st in streams:
            st[7][...] = scores(st, j + 1)
            step(st, st[6][...], j)
        for st in streams:
            st[6][...] = scores(st, j + 2)
            step(st, st[7][...], j + 1)

    def body(jj, carry):
        pair(4 * jj)
        pair(4 * jj + 2)
        return carry

    lax.fori_loop(0, last // 4, body, 0)
    rem = last % 4

    @pl.when(rem >= 2)
    def _():
        pair(last - rem)

    @pl.when(last % 2 == 1)
    def _():
        for st in streams:
            st[7][...] = scores(st, last)
            step(st, st[6][...], last - 1)
        for st in streams:
            step(st, causal(st[7][...], last), last)

    @pl.when(last % 2 == 0)
    def _():
        for st in streams:
            step(st, causal(st[6][...], last), last)

    outs = []
    for st in streams:
        acc = st[5][...]
        outs.append(acc[:, 0:LANES] / acc[:, LANES:2 * LANES])
    return outs


def _nsa_kernel(qn_ref, qr_ref, kc_ref, vc_ref, ks_ref, vs_ref, kw_ref, vw_ref, g_ref, z_ref,
                ovt_ref, eg_ref, y_ref, m_ref, acc_ref, sa_ref, sb_ref):
    i = pl.program_id(1)
    tq, tk = ATT_TQ, ATT_TK
    r = NSA_GROUP
    s0 = i * tq
    t = s0 + lax.broadcasted_iota(jnp.int32, (tq, 1), 0)
    t_all = jnp.concatenate([t] * r, axis=0)
    lane = lax.broadcasted_iota(jnp.int32, (tq, LANES), 1)

    def per_head_bias(s, bias):
        return jnp.concatenate([s[j * tq:(j + 1) * tq] + bias for j in range(r)], axis=0)

    kc_bf = kc_ref[0].astype(BF16)
    n_cmp = kc_bf.shape[0]
    vc_ext = jnp.concatenate([vc_ref[0].astype(BF16), jnp.ones((n_cmp, LANES), BF16)], axis=1)
    cmp_end = lax.broadcasted_iota(jnp.int32, (1, n_cmp), 1) * CMP_STRIDE + (CMP_BLOCK - 1)
    cmp_bias = jnp.where(cmp_end <= t, 0.0, MASK_BIAS)

    win_start = jnp.maximum(s0 - WINDOW, 0)
    n_win = WINDOW + tq
    kwin = kw_ref[0, pl.ds(pl.multiple_of(win_start, tq), n_win), :]
    vwin_ext = jnp.concatenate([vw_ref[0, pl.ds(pl.multiple_of(win_start, tq), n_win), :],
                                jnp.ones((n_win, LANES), BF16)], axis=1)
    kpos_w = win_start + lax.broadcasted_iota(jnp.int32, (1, n_win), 1)
    win_bias = jnp.where((kpos_w <= t) & (kpos_w > t - WINDOW), 0.0, MASK_BIAS)

    blk = lax.broadcasted_iota(jnp.int32, (LANES, tq), 0)
    qblk = _block_of(s0 + lax.broadcasted_iota(jnp.int32, (1, tq), 1), SEL_BLOCK)
    forced = (blk == 0) | (blk == qblk) | (blk == qblk - 1)
    forced_sel = jnp.where(forced, 1.0, 0.0)
    ovt = ovt_ref[...]

    o_cmp, o_win, qs_all = [], [], []
    for g in range(NSA_KV_HEADS):
        heads = [g * r + j for j in range(r)]
        qn = jnp.concatenate([qn_ref[0, :, h * LANES:(h + 1) * LANES] for h in heads], axis=0)
        qr = jnp.concatenate([qr_ref[0, :, h * LANES:(h + 1) * LANES] for h in heads], axis=0)

        e, ov, rinv = _exp2_rows(per_head_bias(_dot_nt(qn, kc_bf), cmp_bias), vc_ext)
        pc = e * jnp.tile(rinv, (1, n_cmp // LANES))
        o_cmp.append(ov * rinv)

        pc_sum = pc[0:tq]
        for j in range(1, r):
            pc_sum = pc_sum + pc[j * tq:(j + 1) * tq]
        imp = _dot_nt(ovt, pc_sum.astype(BF16))
        imp = jnp.where(forced, -jnp.inf, jnp.where(blk > qblk, NEG_BIG, imp))
        sel = _top_k_rows(imp, SEL_TOPK - N_FORCED, forced_sel)
        sel_bias = jnp.where(sel.T > 0.5, 0.0, MASK_BIAS).astype(BF16)
        qs_all.append(jnp.concatenate([qr, jnp.concatenate([sel_bias] * r, axis=0)], axis=1))

        _, ov, rinv = _exp2_rows(per_head_bias(_dot_nt(qr, kwin), win_bias), vwin_ext)
        o_win.append(ov * rinv)

    rows = r * tq
    stream = (jnp.concatenate(qs_all, axis=0), ks_ref, vs_ref, 0, m_ref, acc_ref, sa_ref, sb_ref)
    t_rows = jnp.concatenate([t_all] * NSA_KV_HEADS, axis=0)
    o_both, = _dense_masked_attention([stream], t_rows, (s0 + tq + tk - 1) // tk, SEL_BLOCK)
    o_slc = [o_both[g * rows:(g + 1) * rows] for g in range(NSA_KV_HEADS)]

    gates = _dot(_sigmoid(g_ref[0]).astype(BF16), eg_ref[...])
    z = z_ref[0]
    for p in range(D_NSA // LANES):
        g = (2 * p) // r
        y = None
        for b, o_all in enumerate((o_cmp, o_slc, o_win)):
            o = o_all[g]
            j0, j1 = (2 * p) % r, (2 * p + 1) % r
            a0 = o[j0 * tq:(j0 + 1) * tq]
            a1 = o[j1 * tq:(j1 + 1) * tq]
            if g != 0:
                a0 = pltpu.roll(a0, HEAD_DIM, 1)
            if g != 1:
                a1 = pltpu.roll(a1, HEAD_DIM, 1)
            pair = jnp.where(lane < HEAD_DIM, a0, a1)
            term = gates[:, b * D_NSA + p * LANES:b * D_NSA + (p + 1) * LANES] * pair
            y = term if y is None else y + term
        zp = z[:, p * LANES:(p + 1) * LANES].astype(F32)
        y_ref[0, :, p * LANES:(p + 1) * LANES] = (y * (zp * _sigmoid(zp))).astype(BF16)


def _nsa_attention(qn, qr, kc, vc, ks, vs, kw, vw, gl, z, batch, seq_len):
    tq = ATT_TQ
    n_cmp = seq_len // CMP_STRIDE
    n_sel = seq_len // SEL_BLOCK
    c_start = np.arange(n_cmp)[:, None] * CMP_STRIDE
    s_start = np.arange(LANES)[None, :] * SEL_BLOCK
    overlap = ((c_start < s_start + SEL_BLOCK) & (c_start + CMP_BLOCK > s_start) & (np.arange(LANES)[None, :] < n_sel))
    overlap_t = jnp.asarray(overlap.T.astype(np.float32), dtype=BF16)
    eg = np.zeros((LANES, N_NSA_BRANCHES * D_NSA), np.float32)
    for h in range(NSA_HEADS):
        for b in range(N_NSA_BRANCHES):
            eg[h * N_NSA_BRANCHES + b, b * D_NSA + h * HEAD_DIM:b * D_NSA + (h + 1) * HEAD_DIM] = 1.0
    eg = jnp.asarray(eg, dtype=BF16)

    def r3(a):
        return a.reshape(batch, seq_len, a.shape[-1])

    tile = lambda w: pl.BlockSpec((1, tq, w), lambda b, i: (b, i, 0))
    full = lambda n, w: pl.BlockSpec((1, n, w), lambda b, i: (b, 0, 0))
    const = lambda shape: pl.BlockSpec(shape, lambda b, i: (0, 0))
    rows = NSA_HEADS * tq
    return pl.pallas_call(
        _nsa_kernel,
        grid=(batch, seq_len // tq),
        in_specs=[tile(NSA_HEADS * LANES), tile(NSA_HEADS * LANES), full(n_cmp, D_NSA_KV), full(n_cmp, D_NSA_KV),
                  full(seq_len, D_NSA_KV), full(seq_len, D_NSA_KV), full(seq_len, D_NSA_KV),
                  full(seq_len, D_NSA_KV), tile(LANES), tile(D_NSA),
                  const(overlap_t.shape), const(eg.shape)],
        out_specs=tile(D_NSA),
        out_shape=jax.ShapeDtypeStruct((batch, seq_len, D_NSA), BF16),
        scratch_shapes=[pltpu.VMEM((rows, LANES), F32), pltpu.VMEM((rows, 2 * LANES), F32),
                        pltpu.VMEM((rows, ATT_TK), F32), pltpu.VMEM((rows, ATT_TK), F32)],
        compiler_params=pltpu.CompilerParams(dimension_semantics=("arbitrary", "arbitrary"),
                                             vmem_limit_bytes=VMEM_LIMIT),
        name="nsa_attention",
    )(r3(qn), r3(qr), kc, vc, r3(ks), r3(vs), r3(kw), r3(vw), r3(gl), r3(z), overlap_t, eg)


def _moba_kernel(q_ref, km_ref, mk_ref, mv_ref, z_ref, y_ref, *scratch):
    i = pl.program_id(2)
    tq, tk = MOBA_TQ, ATT_TK
    s0 = i * tq
    t = s0 + lax.broadcasted_iota(jnp.int32, (tq, 1), 0)
    t_all = jnp.concatenate([t, t], axis=0)
    lane = lax.broadcasted_iota(jnp.int32, (tq, LANES), 1)
    blk = lax.broadcasted_iota(jnp.int32, (LANES, tq), 0)
    own = _block_of(s0 + lax.broadcasted_iota(jnp.int32, (1, tq), 1), MOBA_BLOCK)
    nb = km_ref.shape[1]

    streams = []
    for p in range(MOBA_PAIRS):
        q = q_ref[0, :, p * LANES:(p + 1) * LANES]
        km = km_ref[0, :, p * LANES:(p + 1) * LANES]
        if nb < LANES:
            km = jnp.concatenate([km, jnp.zeros((LANES - nb, LANES), F32)], axis=0)
        km_bf = km.astype(BF16)
        qs = []
        for hh in range(2):
            qh = jnp.where((lane >= hh * HEAD_DIM) & (lane < (hh + 1) * HEAD_DIM), q, jnp.zeros_like(q))
            sg = _dot_nt(km_bf, qh)
            chosen = _top_k_rows(jnp.where(blk < own, sg, NEG_BIG), MOBA_TOPK)
            visible = jnp.where(blk < own, chosen, jnp.where(blk == own, 1.0, 0.0))
            bias = jnp.where(visible.T > 0.5, 0.0, MASK_BIAS).astype(BF16)
            qs.append(jnp.concatenate([qh, bias], axis=1))
        streams.append((jnp.concatenate(qs, axis=0), mk_ref, mv_ref, p * LANES) + tuple(scratch[4 * p:4 * p + 4]))

    outs = _dense_masked_attention(streams, t_all, (s0 + tq + tk - 1) // tk, MOBA_BLOCK)
    for p, o in enumerate(outs):
        y = jnp.where(lane < HEAD_DIM, o[0:tq], o[tq:2 * tq])
        z = z_ref[0, :, p * LANES:(p + 1) * LANES].astype(F32)
        y_ref[0, :, p * LANES:(p + 1) * LANES] = (y * (z * _sigmoid(z))).astype(BF16)


def _moba_attention(mq, kmean, mk, mv, mz, batch, seq_len):
    tq = MOBA_TQ
    nb = seq_len // MOBA_BLOCK
    w = MOBA_PAIRS * LANES
    n_steps = D_MOBA // w

    def r3(a):
        return a.reshape(batch, seq_len, a.shape[-1])

    tile = pl.BlockSpec((1, tq, w), lambda b, p, i: (b, i, p))
    full = lambda n, width: pl.BlockSpec((1, n, width), lambda b, p, i: (b, 0, p))
    rows = 2 * tq
    per_pair = [pltpu.VMEM((rows, LANES), F32), pltpu.VMEM((rows, 2 * LANES), F32),
                pltpu.VMEM((rows, ATT_TK), F32), pltpu.VMEM((rows, ATT_TK), F32)]
    return pl.pallas_call(
        _moba_kernel,
        grid=(batch, n_steps, seq_len // tq),
        in_specs=[tile, full(nb, w), full(seq_len, w), full(seq_len, w), tile],
        out_specs=tile,
        out_shape=jax.ShapeDtypeStruct((batch, seq_len, D_MOBA), BF16),
        scratch_shapes=per_pair * MOBA_PAIRS,
        compiler_params=pltpu.CompilerParams(dimension_semantics=("arbitrary", "arbitrary", "arbitrary"),
                                             vmem_limit_bytes=VMEM_LIMIT),
        name="moba_attention",
    )(r3(mq), kmean.reshape(batch, nb, D_MOBA), r3(mk), r3(mv), r3(mz))


def _out_kernel(x_ref, yn_ref, ym_ref, w_ref, gain_ref, bias_ref, o_ref, wbf_ref):
    @pl.when(pl.program_id(0) == 0)
    def _():
        wbf_ref[...] = w_ref[...].astype(BF16)

    sub = _dot(yn_ref[...], wbf_ref[0:D_NSA, :]) + _dot(ym_ref[...], wbf_ref[D_NSA:D_NSA + D_MOBA, :])
    h = DEEPNORM_ALPHA * x_ref[...] + sub
    mu = jnp.mean(h, axis=1, keepdims=True)
    d = h - mu
    var = jnp.mean(d * d, axis=1, keepdims=True)
    o_ref[...] = d * lax.rsqrt(var + LN_EPS) * gain_ref[...] + bias_ref[...]


def _out_project(x2, y_nsa, y_moba, w_out, gain, bias):
    m = x2.shape[0]
    tm = PROJ_TM
    d_mix = D_NSA + D_MOBA
    return pl.pallas_call(
        _out_kernel,
        grid=(m // tm,),
        in_specs=[_row_spec(tm, D_MODEL), _row_spec(tm, D_NSA), _row_spec(tm, D_MOBA),
                  pl.BlockSpec((d_mix, D_MODEL), lambda i: (0, 0)),
                  pl.BlockSpec((1, D_MODEL), lambda i: (0, 0)), pl.BlockSpec((1, D_MODEL), lambda i: (0, 0))],
        out_specs=_row_spec(tm, D_MODEL),
        out_shape=jax.ShapeDtypeStruct((m, D_MODEL), F32),
        scratch_shapes=[pltpu.VMEM((d_mix, D_MODEL), BF16)],
        compiler_params=pltpu.CompilerParams(dimension_semantics=("arbitrary",), vmem_limit_bytes=VMEM_LIMIT),
        name="out_proj_norm",
    )(x2, y_nsa, y_moba, w_out, gain.reshape(1, D_MODEL), bias.reshape(1, D_MODEL))


def _rope_tables(seq_len):
    half = ROPE_DIM // 2
    inv_freq = ROPE_THETA ** (-jnp.arange(0, ROPE_DIM, 2, dtype=F32) / ROPE_DIM)
    ang = jnp.arange(seq_len, dtype=F32)[:, None] * inv_freq[None, :]
    cos, sin = jnp.cos(ang), jnp.sin(ang)
    zeros = jnp.zeros((seq_len, HEAD_DIM - ROPE_DIM), F32)
    zh = jnp.zeros((seq_len, half), F32)
    c = jnp.concatenate([cos, cos, zeros + 1.0], axis=1)
    sm = jnp.concatenate([-sin, zh, zeros], axis=1)
    sp = jnp.concatenate([zh, sin, zeros], axis=1)
    return tuple(jnp.tile(a, (1, LANES // HEAD_DIM)) for a in (c, sm, sp))


def _layer(x, w_in, cmp_pos_k, cmp_w1_k, cmp_b1_k, cmp_w2_k, cmp_pos_v, cmp_w1_v, cmp_b1_v, cmp_w2_v,
           w_out, ln_gain, ln_bias, tables):
    batch, seq_len, _ = x.shape
    kv = D_NSA_KV
    sizes = (D_NSA, kv, kv, kv, kv, kv, kv, N_NSA_BRANCHES * NSA_HEADS, D_NSA, D_MOBA, D_MOBA, D_MOBA, D_MOBA)
    offs = np.concatenate([[0], np.cumsum(sizes)])
    col = lambda s: w_in[:, int(offs[s]):int(offs[s + 1])]
    zero_half = jnp.zeros((D_MODEL, HEAD_DIM), F32)
    q_blocks = []
    for h in range(NSA_HEADS):
        wq_h = col(0)[:, h * HEAD_DIM:(h + 1) * HEAD_DIM]
        q_blocks += [wq_h, zero_half] if h // NSA_GROUP == 0 else [zero_half, wq_h]
    gate_pad = jnp.zeros((D_MODEL, LANES - sizes[7]), F32)
    w_all = jnp.concatenate(q_blocks + [col(1), col(9), col(10),
                                        col(2), col(3), col(4), col(5), col(6), col(7), gate_pad,
                                        col(8), col(11), col(12)], axis=1).astype(BF16)

    x2 = x.reshape(batch * seq_len, D_MODEL)
    (qn, qr, kc_raw, mq, mk, kmean, vc_raw, ks, vs, kw, vw, gl, z, mv, mz) = _project(x2, w_all, tables, seq_len)
    kc = _compress(kc_raw, cmp_pos_k, cmp_w1_k, cmp_b1_k, cmp_w2_k, batch, seq_len)
    vc = _compress(vc_raw, cmp_pos_v, cmp_w1_v, cmp_b1_v, cmp_w2_v, batch, seq_len)
    y_nsa = _nsa_attention(qn, qr, kc, vc, ks, vs, kw, vw, gl, z, batch, seq_len)
    y_moba = _moba_attention(mq, kmean, mk, mv, mz, batch, seq_len)
    out = _out_project(x2, y_nsa.reshape(batch * seq_len, D_NSA), y_moba.reshape(batch * seq_len, D_MOBA),
                       w_out, ln_gain, ln_bias)
    return out.reshape(batch, seq_len, D_MODEL)


def kernel(x, w_in, cmp_pos_k, cmp_w1_k, cmp_b1_k, cmp_w2_k, cmp_pos_v, cmp_w1_v, cmp_b1_v, cmp_w2_v,
           w_out, ln_gain, ln_bias):
    tables = _rope_tables(x.shape[1])
    h = x
    for layer in range(w_in.shape[0]):
        h = _layer(h, w_in[layer], cmp_pos_k[layer], cmp_w1_k[layer], cmp_b1_k[layer], cmp_w2_k[layer],
                   cmp_pos_v[layer], cmp_w1_v[layer], cmp_b1_v[layer], cmp_w2_v[layer],
                   w_out[layer], ln_gain[layer], ln_bias[layer], tables)
    return h
```

```python
import numpy as np
import jax
import jax.numpy as jnp
from jax import lax
from jax.experimental import pallas as pl
from jax.experimental.pallas import tpu as pltpu

D_MODEL = 1024
HEAD_DIM = 64
NSA_HEADS = 8
NSA_KV_HEADS = 2
NSA_GROUP = NSA_HEADS // NSA_KV_HEADS
MOBA_HEADS = 8
D_NSA = NSA_HEADS * HEAD_DIM
D_NSA_KV = NSA_KV_HEADS * HEAD_DIM
D_MOBA = MOBA_HEADS * HEAD_DIM
N_NSA_BRANCHES = 3
ROPE_THETA = 500000.0
ROPE_DIM = HEAD_DIM // 4
CMP_BLOCK = 32
CMP_STRIDE = 16
CMP_HIDDEN = 4 * HEAD_DIM
SEL_BLOCK = 64
SEL_TOPK = 16
N_FORCED = 3
WINDOW = 512
MOBA_BLOCK = 256
MOBA_TOPK = 3
LN_EPS = 1e-5
DEPTH = 1
DEEPNORM_ALPHA = (2.0 * DEPTH) ** 0.25
NEG_BIG = -1e30
POS_BIG = 1e30
MASK_BIAS = -1e30
M_INIT = -1e29
LOG2E = 1.4426950408889634
Q_SCALE = HEAD_DIM ** -0.5 * LOG2E

LANES = 128
PROJ_TM = 512
ATT_TQ = 256
MOBA_TQ = 512
MOBA_PAIRS = 2
ATT_TK = 512
VMEM_LIMIT = 52 * 1024 * 1024

F32 = jnp.float32
BF16 = jnp.bfloat16
NT_DIMS = (((1,), (1,)), ((), ()))


def _dot(a, b):
    return jnp.dot(a, b, preferred_element_type=F32)


def _dot_nt(a, b):
    return lax.dot_general(a, b, NT_DIMS, preferred_element_type=F32)


def _rope(a, c, sm, sp):
    outs = []
    for p in range(a.shape[1] // LANES):
        s = a[:, p * LANES:(p + 1) * LANES]
        outs.append(s * c + pltpu.roll(s, LANES - ROPE_DIM // 2, 1) * sm + pltpu.roll(s, ROPE_DIM // 2, 1) * sp)
    return outs[0] if len(outs) == 1 else jnp.concatenate(outs, axis=1)


def _sigmoid(x):
    return 1.0 / (1.0 + jnp.exp(-x))


def _block_of(t, block):
    assert block & (block - 1) == 0
    return lax.shift_right_logical(t, jnp.int32(block.bit_length() - 1))


def _proj_kernel(x_ref, w_ref, c_ref, sm_ref, sp_ref,
                 qn_ref, qr_ref, kc_ref, mq_ref, mk_ref, km_ref,
                 vc_ref, ks_ref, vs_ref, kw_ref, vw_ref, g_ref, z_ref, mv_ref, mz_ref):
    tm = x_ref.shape[0]
    xb = x_ref[...].astype(BF16)
    c, sm, sp = c_ref[...], sm_ref[...], sp_ref[...]
    nq = NSA_HEADS * LANES
    kv = D_NSA_KV
    n_first = nq + kv + 2 * D_MOBA

    acc = _dot(xb, w_ref[:, 0:n_first])
    q = acc[:, 0:nq] * Q_SCALE
    qn_ref[...] = q.astype(BF16)
    qr_ref[...] = _rope(q, c, sm, sp).astype(BF16)
    kc_ref[...] = acc[:, nq:nq + kv]
    o = nq + kv
    mq_ref[...] = _rope(acc[:, o:o + D_MOBA] * Q_SCALE, c, sm, sp).astype(BF16)
    mk = _rope(acc[:, o + D_MOBA:o + 2 * D_MOBA], c, sm, sp)
    mk_ref[...] = mk.T.astype(BF16)
    for r in range(tm // MOBA_BLOCK):
        blk = mk[r * MOBA_BLOCK:(r + 1) * MOBA_BLOCK]
        km_ref[r] = jnp.sum(blk, axis=0, keepdims=True) / MOBA_BLOCK

    acc = _dot(xb, w_ref[:, n_first:])
    vc_ref[...] = acc[:, 0:kv]
    ks_ref[...] = _rope(acc[:, kv:2 * kv], c, sm, sp).T.astype(BF16)
    vs_ref[...] = acc[:, 2 * kv:3 * kv].astype(BF16)
    kw_ref[...] = _rope(acc[:, 3 * kv:4 * kv], c, sm, sp).astype(BF16)
    vw_ref[...] = acc[:, 4 * kv:5 * kv].astype(BF16)
    g_ref[...] = acc[:, 5 * kv:6 * kv]
    o = 6 * kv
    z_ref[...] = acc[:, o:o + D_NSA].astype(BF16)
    mv_ref[...] = acc[:, o + D_NSA:o + D_NSA + D_MOBA].astype(BF16)
    mz_ref[...] = acc[:, o + D_NSA + D_MOBA:o + D_NSA + 2 * D_MOBA].astype(BF16)


def _row_spec(tm, width):
    return pl.BlockSpec((tm, width), lambda i: (i, 0))


def _project(x2, w_all, tables, seq_len):
    m = x2.shape[0]
    tm = PROJ_TM
    nt = seq_len // tm
    nq = NSA_HEADS * LANES
    table_spec = pl.BlockSpec((tm, LANES), lambda i: (i % nt, 0))
    widths = [(nq, BF16), (nq, BF16), (D_NSA_KV, F32), (D_MOBA, BF16), (D_MOBA, "T"), None,
              (D_NSA_KV, F32), (D_NSA_KV, "T"), (D_NSA_KV, BF16), (D_NSA_KV, BF16), (D_NSA_KV, BF16),
              (LANES, F32), (D_NSA, BF16), (D_MOBA, BF16), (D_MOBA, BF16)]
    out_specs, out_shape = [], []
    for wd in widths:
        if wd is None:
            out_specs.append(pl.BlockSpec((tm // MOBA_BLOCK, 1, D_MOBA), lambda i: (i, 0, 0)))
            out_shape.append(jax.ShapeDtypeStruct((m // MOBA_BLOCK, 1, D_MOBA), F32))
        elif wd[1] == "T":
            out_specs.append(pl.BlockSpec((wd[0], tm), lambda i: (0, i)))
            out_shape.append(jax.ShapeDtypeStruct((wd[0], m), BF16))
        else:
            out_specs.append(_row_spec(tm, wd[0]))
            out_shape.append(jax.ShapeDtypeStruct((m, wd[0]), wd[1]))
    return pl.pallas_call(
        _proj_kernel,
        grid=(m // tm,),
        in_specs=[_row_spec(tm, D_MODEL),
                  pl.BlockSpec(w_all.shape, lambda i: (0, 0), pipeline_mode=pl.Buffered(1))] + [table_spec] * 3,
        out_specs=out_specs,
        out_shape=out_shape,
        compiler_params=pltpu.CompilerParams(dimension_semantics=("arbitrary",), vmem_limit_bytes=VMEM_LIMIT),
        name="proj",
    )(x2, w_all, *tables)


def _compress_kernel(raw_ref, pos_ref, w1_ref, b1_ref, w2_ref, out_ref, wbd_ref):
    n = raw_ref.shape[0] // CMP_STRIDE

    @pl.when(pl.program_id(0) == 0)
    def _():
        wbd_ref[...] = jnp.zeros(wbd_ref.shape, BF16)
        for l in range(CMP_BLOCK):
            w = w1_ref[l * HEAD_DIM:(l + 1) * HEAD_DIM, :].astype(BF16)
            for g in range(NSA_KV_HEADS):
                wbd_ref[l, g * HEAD_DIM:(g + 1) * HEAD_DIM, g * CMP_HIDDEN:(g + 1) * CMP_HIDDEN] = w

    first, second = None, None
    for l in range(CMP_STRIDE):
        a = raw_ref[pl.ds(l, n, stride=CMP_STRIDE), :]
        f = _dot((a + pos_ref[l]).astype(BF16), wbd_ref[l])
        s = _dot((a + pos_ref[l + CMP_STRIDE]).astype(BF16), wbd_ref[l + CMP_STRIDE])
        first = f if first is None else first + f
        second = s if second is None else second + s
    nxt = pltpu.roll(second, n - 1, 0)
    row = lax.broadcasted_iota(jnp.int32, nxt.shape, 0)
    h = first + jnp.where(row < n - 1, nxt, 0.0) + b1_ref[...]
    h = h * _sigmoid(h)
    out_ref[0] = _dot(h.astype(BF16), w2_ref[...].astype(BF16))


def _compress(raw, pos, w1, b1, w2, batch, seq_len):
    g = NSA_KV_HEADS
    n_chunks = seq_len // CMP_STRIDE
    posb = jnp.tile(pos.reshape(CMP_BLOCK, 1, HEAD_DIM), (1, 1, g))
    b1b = jnp.tile(b1.reshape(1, CMP_HIDDEN), (1, g))
    w2b = jnp.einsum('hd,gk->ghkd', w2, jnp.eye(g, dtype=F32)).reshape(g * CMP_HIDDEN, g * HEAD_DIM)
    return pl.pallas_call(
        _compress_kernel,
        grid=(batch,),
        in_specs=[pl.BlockSpec((seq_len, D_NSA_KV), lambda b: (b, 0)),
                  pl.BlockSpec((CMP_BLOCK, 1, D_NSA_KV), lambda b: (0, 0, 0)),
                  pl.BlockSpec((CMP_BLOCK * HEAD_DIM, CMP_HIDDEN), lambda b: (0, 0)),
                  pl.BlockSpec((1, g * CMP_HIDDEN), lambda b: (0, 0)),
                  pl.BlockSpec((g * CMP_HIDDEN, g * HEAD_DIM), lambda b: (0, 0))],
        out_specs=pl.BlockSpec((1, n_chunks, g * HEAD_DIM), lambda b: (b, 0, 0)),
        out_shape=jax.ShapeDtypeStruct((batch, n_chunks, g * HEAD_DIM), F32),
        scratch_shapes=[pltpu.VMEM((CMP_BLOCK, D_NSA_KV, g * CMP_HIDDEN), BF16)],
        compiler_params=pltpu.CompilerParams(dimension_semantics=("arbitrary",), vmem_limit_bytes=VMEM_LIMIT),
        name="compress",
    )(raw, posb, w1, b1b, w2b)


def _top_k_rows(val, k, sel=None):
    row = lax.broadcasted_iota(jnp.int32, val.shape, 0).astype(F32)
    sel = jnp.zeros(val.shape, F32) if sel is None else sel
    for _ in range(k):
        m = jnp.max(val, axis=0, keepdims=True)
        idx = jnp.min(jnp.where(val == m, row, float(LANES)), axis=0, keepdims=True)
        hit = row == idx
        sel = jnp.where(hit, 1.0, sel)
        val = jnp.where(hit, -jnp.inf, val)
    return sel


def _exp2_rows(s, v_ext):
    n = s.shape[1] // LANES
    part = s[:, 0:LANES]
    for c in range(1, n):
        part = jnp.maximum(part, s[:, c * LANES:(c + 1) * LANES])
    m = jnp.maximum(jnp.max(part, axis=1, keepdims=True), M_INIT)
    e = jnp.exp2(s - m)
    ov = _dot(e.astype(BF16), v_ext)
    d = ov[:, LANES:2 * LANES]
    return e, ov[:, 0:LANES], 1.0 / jnp.where(d > 0, d, 1.0)


def _online_step(s, v_ext, m_ref, acc_ref):
    tk = s.shape[1]
    part = s[:, 0:LANES]
    for c in range(1, tk // LANES):
        part = jnp.maximum(part, s[:, c * LANES:(c + 1) * LANES])
    m_prev = m_ref[...]
    m_new = jnp.maximum(m_prev, jnp.max(part, axis=1, keepdims=True))
    alpha = jnp.exp2(m_prev - m_new)
    p = jnp.exp2(s - jnp.tile(m_new, (1, tk // LANES)))
    acc_ref[...] = jnp.tile(alpha, (1, 2)) * acc_ref[...] + _dot(p.astype(BF16), v_ext)
    m_ref[...] = m_new


def _dense_masked_attention(streams, t_all, n_chunks, block_size):
    tk = ATT_TK

    block_id = lax.broadcasted_iota(jnp.int32, (LANES, tk), 0)
    key_pos = lax.broadcasted_iota(jnp.int32, (LANES, tk), 1)
    ones = jnp.ones((tk, LANES), BF16)

    def scores(st, j):
        qs, kt_ref, _, col = st[0:4]
        kt = kt_ref[col:col + LANES, pl.ds(pl.multiple_of(j * tk, tk), tk)]
        onehot = jnp.where(block_id == _block_of(j * tk + key_pos, block_size), 1.0, 0.0).astype(BF16)
        return _dot(qs, jnp.concatenate([kt, onehot], axis=0))

    def step(st, s, j):
        _, _, v_ref, col, m_ref, acc_ref = st[0:6]
        v = v_ref[0, pl.ds(pl.multiple_of(j * tk, tk), tk), col:col + LANES]
        _online_step(s, jnp.concatenate([v, ones], axis=1), m_ref, acc_ref)

    def causal(s, j):
        kpos = j * tk + lax.broadcasted_iota(jnp.int32, (1, tk), 1)
        return jnp.where(kpos <= t_all, s, MASK_BIAS)

    last = n_chunks - 1
    for st in streams:
        m_ref, acc_ref, sa_ref = st[4], st[5], st[6]
        m_ref[...] = jnp.full(m_ref.shape, M_INIT, F32)
        acc_ref[...] = jnp.zeros(acc_ref.shape, F32)
        sa_ref[...] = scores(st, 0)

    def pair(j):
        for st in streams:
            st[7][...] = scores(st, j + 1)
            step(st, st[6][...], j)
        for st in streams:
            st[6][...] = scores(st, j + 2)
            step(st, st[7][...], j + 1)

    def body(jj, carry):
        pair(4 * jj)
        pair(4 * jj + 2)
        return carry

    lax.fori_loop(0, last // 4, body, 0)
    rem = last % 4

    @pl.when(rem >= 2)
    def _():
        pair(last - rem)

    @pl.when(last % 2 == 1)
    def _():
        for st in streams:
            st[7][...] = scores(st, last)
            step(st, st[6][...], last - 1)
        for st in streams:
            step(st, causal(st[7][...], last), last)

    @pl.when(last % 2 == 0)
    def _():
        for st in streams:
            step(st, causal(st[6][...], last), last)

    outs = []
    for st in streams:
        acc = st[5][...]
        outs.append(acc[:, 0:LANES] / acc[:, LANES:2 * LANES])
    return outs


def _nsa_kernel(qn_ref, qr_ref, kc_ref, vc_ref, ks_ref, vs_ref, kw_ref, vw_ref, g_ref, z_ref,
                ovt_ref, eg_ref, y_ref, m_ref, acc_ref, sa_ref, sb_ref):
    i = pl.program_id(1)
    tq, tk = ATT_TQ, ATT_TK
    r = NSA_GROUP
    s0 = i * tq
    t = s0 + lax.broadcasted_iota(jnp.int32, (tq, 1), 0)
    t_all = jnp.concatenate([t] * r, axis=0)
    lane = lax.broadcasted_iota(jnp.int32, (tq, LANES), 1)

    def per_head_bias(s, bias):
        return jnp.concatenate([s[j * tq:(j + 1) * tq] + bias for j in range(r)], axis=0)

    kc_bf = kc_ref[0].astype(BF16)
    n_cmp = kc_bf.shape[0]
    vc_ext = jnp.concatenate([vc_ref[0].astype(BF16), jnp.ones((n_cmp, LANES), BF16)], axis=1)
    cmp_end = lax.broadcasted_iota(jnp.int32, (1, n_cmp), 1) * CMP_STRIDE + (CMP_BLOCK - 1)
    cmp_bias = jnp.where(cmp_end <= t, 0.0, MASK_BIAS)

    win_start = jnp.maximum(s0 - WINDOW, 0)
    n_win = WINDOW + tq
    kwin = kw_ref[0, pl.ds(pl.multiple_of(win_start, tq), n_win), :]
    vwin_ext = jnp.concatenate([vw_ref[0, pl.ds(pl.multiple_of(win_start, tq), n_win), :],
                                jnp.ones((n_win, LANES), BF16)], axis=1)
    kpos_w = win_start + lax.broadcasted_iota(jnp.int32, (1, n_win), 1)
    win_bias = jnp.where((kpos_w <= t) & (kpos_w > t - WINDOW), 0.0, MASK_BIAS)

    blk = lax.broadcasted_iota(jnp.int32, (LANES, tq), 0)
    qblk = _block_of(s0 + lax.broadcasted_iota(jnp.int32, (1, tq), 1), SEL_BLOCK)
    forced = (blk == 0) | (blk == qblk) | (blk == qblk - 1)
    forced_sel = jnp.where(forced, 1.0, 0.0)
    ovt = ovt_ref[...]

    o_cmp, o_win, qs_all = [], [], []
    for g in range(NSA_KV_HEADS):
        heads = [g * r + j for j in range(r)]
        qn = jnp.concatenate([qn_ref[0, :, h * LANES:(h + 1) * LANES] for h in heads], axis=0)
        qr = jnp.concatenate([qr_ref[0, :, h * LANES:(h + 1) * LANES] for h in heads], axis=0)

        e, ov, rinv = _exp2_rows(per_head_bias(_dot_nt(qn, kc_bf), cmp_bias), vc_ext)
        pc = e * jnp.tile(rinv, (1, n_cmp // LANES))
        o_cmp.append(ov * rinv)

        pc_sum = pc[0:tq]
        for j in range(1, r):
            pc_sum = pc_sum + pc[j * tq:(j + 1) * tq]
        imp = _dot_nt(ovt, pc_sum.astype(BF16))
        imp = jnp.where(forced, -jnp.inf, jnp.where(blk > qblk, NEG_BIG, imp))
        sel = _top_k_rows(imp, SEL_TOPK - N_FORCED, forced_sel)
        sel_bias = jnp.where(sel.T > 0.5, 0.0, MASK_BIAS).astype(BF16)
        qs_all.append(jnp.concatenate([qr, jnp.concatenate([sel_bias] * r, axis=0)], axis=1))

        _, ov, rinv = _exp2_rows(per_head_bias(_dot_nt(qr, kwin), win_bias), vwin_ext)
        o_win.append(ov * rinv)

    rows = r * tq
    stream = (jnp.concatenate(qs_all, axis=0), ks_ref, vs_ref, 0, m_ref, acc_ref, sa_ref, sb_ref)
    t_rows = jnp.concatenate([t_all] * NSA_KV_HEADS, axis=0)
    o_both, = _dense_masked_attention([stream], t_rows, (s0 + tq + tk - 1) // tk, SEL_BLOCK)
    o_slc = [o_both[g * rows:(g + 1) * rows] for g in range(NSA_KV_HEADS)]

    gates = _dot(_sigmoid(g_ref[0]).astype(BF16), eg_ref[...])
    z = z_ref[0]
    for p in range(D_NSA // LANES):
        g = (2 * p) // r
        y = None
        for b, o_all in enumerate((o_cmp, o_slc, o_win)):
            o = o_all[g]
            j0, j1 = (2 * p) % r, (2 * p + 1) % r
            a0 = o[j0 * tq:(j0 + 1) * tq]
            a1 = o[j1 * tq:(j1 + 1) * tq]
            if g != 0:
                a0 = pltpu.roll(a0, HEAD_DIM, 1)
            if g != 1:
                a1 = pltpu.roll(a1, HEAD_DIM, 1)
            pair = jnp.where(lane < HEAD_DIM, a0, a1)
            term = gates[:, b * D_NSA + p * LANES:b * D_NSA + (p + 1) * LANES] * pair
            y = term if y is None else y + term
        zp = z[:, p * LANES:(p + 1) * LANES].astype(F32)
        y_ref[0, :, p * LANES:(p + 1) * LANES] = (y * (zp * _sigmoid(zp))).astype(BF16)


def _nsa_attention(qn, qr, kc, vc, ks, vs, kw, vw, gl, z, batch, seq_len):
    tq = ATT_TQ
    n_cmp = seq_len // CMP_STRIDE
    n_sel = seq_len // SEL_BLOCK
    c_start = np.arange(n_cmp)[:, None] * CMP_STRIDE
    s_start = np.arange(LANES)[None, :] * SEL_BLOCK
    overlap = ((c_start < s_start + SEL_BLOCK) & (c_start + CMP_BLOCK > s_start) & (np.arange(LANES)[None, :] < n_sel))
    overlap_t = jnp.asarray(overlap.T.astype(np.float32), dtype=BF16)
    eg = np.zeros((LANES, N_NSA_BRANCHES * D_NSA), np.float32)
    for h in range(NSA_HEADS):
        for b in range(N_NSA_BRANCHES):
            eg[h * N_NSA_BRANCHES + b, b * D_NSA + h * HEAD_DIM:b * D_NSA + (h + 1) * HEAD_DIM] = 1.0
    eg = jnp.asarray(eg, dtype=BF16)

    def r3(a):
        return a.reshape(batch, seq_len, a.shape[-1])

    tile = lambda w: pl.BlockSpec((1, tq, w), lambda b, i: (b, i, 0))
    full = lambda n, w: pl.BlockSpec((1, n, w), lambda b, i: (b, 0, 0))
    const = lambda shape: pl.BlockSpec(shape, lambda b, i: (0, 0))
    rows = NSA_HEADS * tq
    return pl.pallas_call(
        _nsa_kernel,
        grid=(batch, seq_len // tq),
        in_specs=[tile(NSA_HEADS * LANES), tile(NSA_HEADS * LANES), full(n_cmp, D_NSA_KV), full(n_cmp, D_NSA_KV),
                  pl.BlockSpec((D_NSA_KV, seq_len), lambda b, i: (0, b)), full(seq_len, D_NSA_KV), full(seq_len, D_NSA_KV),
                  full(seq_len, D_NSA_KV), tile(LANES), tile(D_NSA),
                  const(overlap_t.shape), const(eg.shape)],
        out_specs=tile(D_NSA),
        out_shape=jax.ShapeDtypeStruct((batch, seq_len, D_NSA), BF16),
        scratch_shapes=[pltpu.VMEM((rows, LANES), F32), pltpu.VMEM((rows, 2 * LANES), F32),
                        pltpu.VMEM((rows, ATT_TK), F32), pltpu.VMEM((rows, ATT_TK), F32)],
        compiler_params=pltpu.CompilerParams(dimension_semantics=("arbitrary", "arbitrary"),
                                             vmem_limit_bytes=VMEM_LIMIT),
        name="nsa_attention",
    )(r3(qn), r3(qr), kc, vc, ks, r3(vs), r3(kw), r3(vw), r3(gl), r3(z), overlap_t, eg)


def _moba_kernel(q_ref, km_ref, mk_ref, mv_ref, z_ref, y_ref, *scratch):
    i = pl.program_id(2)
    tq, tk = MOBA_TQ, ATT_TK
    s0 = i * tq
    t = s0 + lax.broadcasted_iota(jnp.int32, (tq, 1), 0)
    t_all = jnp.concatenate([t, t], axis=0)
    lane = lax.broadcasted_iota(jnp.int32, (tq, LANES), 1)
    blk = lax.broadcasted_iota(jnp.int32, (LANES, tq), 0)
    own = _block_of(s0 + lax.broadcasted_iota(jnp.int32, (1, tq), 1), MOBA_BLOCK)
    nb = km_ref.shape[1]

    streams = []
    for p in range(MOBA_PAIRS):
        q = q_ref[0, :, p * LANES:(p + 1) * LANES]
        km = km_ref[0, :, p * LANES:(p + 1) * LANES]
        if nb < LANES:
            km = jnp.concatenate([km, jnp.zeros((LANES - nb, LANES), F32)], axis=0)
        km_bf = km.astype(BF16)
        qs = []
        for hh in range(2):
            qh = jnp.where((lane >= hh * HEAD_DIM) & (lane < (hh + 1) * HEAD_DIM), q, jnp.zeros_like(q))
            sg = _dot_nt(km_bf, qh)
            chosen = _top_k_rows(jnp.where(blk < own, sg, NEG_BIG), MOBA_TOPK)
            visible = jnp.where(blk < own, chosen, jnp.where(blk == own, 1.0, 0.0))
            bias = jnp.where(visible.T > 0.5, 0.0, MASK_BIAS).astype(BF16)
            qs.append(jnp.concatenate([qh, bias], axis=1))
        streams.append((jnp.concatenate(qs, axis=0), mk_ref, mv_ref, p * LANES) + tuple(scratch[4 * p:4 * p + 4]))

    outs = _dense_masked_attention(streams, t_all, (s0 + tq + tk - 1) // tk, MOBA_BLOCK)
    for p, o in enumerate(outs):
        y = jnp.where(lane < HEAD_DIM, o[0:tq], o[tq:2 * tq])
        z = z_ref[0, :, p * LANES:(p + 1) * LANES].astype(F32)
        y_ref[0, :, p * LANES:(p + 1) * LANES] = (y * (z * _sigmoid(z))).astype(BF16)


def _moba_attention(mq, kmean, mk, mv, mz, batch, seq_len):
    tq = MOBA_TQ
    nb = seq_len // MOBA_BLOCK
    w = MOBA_PAIRS * LANES
    n_steps = D_MOBA // w

    def r3(a):
        return a.reshape(batch, seq_len, a.shape[-1])

    tile = pl.BlockSpec((1, tq, w), lambda b, p, i: (b, i, p))
    full = lambda n, width: pl.BlockSpec((1, n, width), lambda b, p, i: (b, 0, p))
    rows = 2 * tq
    per_pair = [pltpu.VMEM((rows, LANES), F32), pltpu.VMEM((rows, 2 * LANES), F32),
                pltpu.VMEM((rows, ATT_TK), F32), pltpu.VMEM((rows, ATT_TK), F32)]
    return pl.pallas_call(
        _moba_kernel,
        grid=(batch, n_steps, seq_len // tq),
        in_specs=[tile, full(nb, w), pl.BlockSpec((w, seq_len), lambda b, p, i: (p, b)), full(seq_len, w), tile],
        out_specs=tile,
        out_shape=jax.ShapeDtypeStruct((batch, seq_len, D_MOBA), BF16),
        scratch_shapes=per_pair * MOBA_PAIRS,
        compiler_params=pltpu.CompilerParams(dimension_semantics=("arbitrary", "arbitrary", "arbitrary"),
                                             vmem_limit_bytes=VMEM_LIMIT),
        name="moba_attention",
    )(r3(mq), kmean.reshape(batch, nb, D_MOBA), mk, r3(mv), r3(mz))


def _out_kernel(x_ref, yn_ref, ym_ref, w_ref, gain_ref, bias_ref, o_ref, wbf_ref):
    @pl.when(pl.program_id(0) == 0)
    def _():
        wbf_ref[...] = w_ref[...].astype(BF16)

    sub = _dot(yn_ref[...], wbf_ref[0:D_NSA, :]) + _dot(ym_ref[...], wbf_ref[D_NSA:D_NSA + D_MOBA, :])
    h = DEEPNORM_ALPHA * x_ref[...] + sub
    mu = jnp.mean(h, axis=1, keepdims=True)
    d = h - mu
    var = jnp.mean(d * d, axis=1, keepdims=True)
    o_ref[...] = d * lax.rsqrt(var + LN_EPS) * gain_ref[...] + bias_ref[...]


def _out_project(x2, y_nsa, y_moba, w_out, gain, bias):
    m = x2.shape[0]
    tm = PROJ_TM
    d_mix = D_NSA + D_MOBA
    return pl.pallas_call(
        _out_kernel,
        grid=(m // tm,),
        in_specs=[_row_spec(tm, D_MODEL), _row_spec(tm, D_NSA), _row_spec(tm, D_MOBA),
                  pl.BlockSpec((d_mix, D_MODEL), lambda i: (0, 0)),
                  pl.BlockSpec((1, D_MODEL), lambda i: (0, 0)), pl.BlockSpec((1, D_MODEL), lambda i: (0, 0))],
        out_specs=_row_spec(tm, D_MODEL),
        out_shape=jax.ShapeDtypeStruct((m, D_MODEL), F32),
        scratch_shapes=[pltpu.VMEM((d_mix, D_MODEL), BF16)],
        compiler_params=pltpu.CompilerParams(dimension_semantics=("arbitrary",), vmem_limit_bytes=VMEM_LIMIT),
        name="out_proj_norm",
    )(x2, y_nsa, y_moba, w_out, gain.reshape(1, D_MODEL), bias.reshape(1, D_MODEL))


def _rope_tables(seq_len):
    half = ROPE_DIM // 2
    inv_freq = ROPE_THETA ** (-jnp.arange(0, ROPE_DIM, 2, dtype=F32) / ROPE_DIM)
    ang = jnp.arange(seq_len, dtype=F32)[:, None] * inv_freq[None, :]
    cos, sin = jnp.cos(ang), jnp.sin(ang)
    zeros = jnp.zeros((seq_len, HEAD_DIM - ROPE_DIM), F32)
    zh = jnp.zeros((seq_len, half), F32)
    c = jnp.concatenate([cos, cos, zeros + 1.0], axis=1)
    sm = jnp.concatenate([-sin, zh, zeros], axis=1)
    sp = jnp.concatenate([zh, sin, zeros], axis=1)
    return tuple(jnp.tile(a, (1, LANES // HEAD_DIM)) for a in (c, sm, sp))


def _layer(x, w_in, cmp_pos_k, cmp_w1_k, cmp_b1_k, cmp_w2_k, cmp_pos_v, cmp_w1_v, cmp_b1_v, cmp_w2_v,
           w_out, ln_gain, ln_bias, tables):
    batch, seq_len, _ = x.shape
    kv = D_NSA_KV
    sizes = (D_NSA, kv, kv, kv, kv, kv, kv, N_NSA_BRANCHES * NSA_HEADS, D_NSA, D_MOBA, D_MOBA, D_MOBA, D_MOBA)
    offs = np.concatenate([[0], np.cumsum(sizes)])
    col = lambda s: w_in[:, int(offs[s]):int(offs[s + 1])]
    zero_half = jnp.zeros((D_MODEL, HEAD_DIM), F32)
    q_blocks = []
    for h in range(NSA_HEADS):
        wq_h = col(0)[:, h * HEAD_DIM:(h + 1) * HEAD_DIM]
        q_blocks += [wq_h, zero_half] if h // NSA_GROUP == 0 else [zero_half, wq_h]
    gate_pad = jnp.zeros((D_MODEL, LANES - sizes[7]), F32)
    w_all = jnp.concatenate(q_blocks + [col(1), col(9), col(10),
                                        col(2), col(3), col(4), col(5), col(6), col(7), gate_pad,
                                        col(8), col(11), col(12)], axis=1).astype(BF16)

    x2 = x.reshape(batch * seq_len, D_MODEL)
    (qn, qr, kc_raw, mq, mk, kmean, vc_raw, ks, vs, kw, vw, gl, z, mv, mz) = _project(x2, w_all, tables, seq_len)
    kc = _compress(kc_raw, cmp_pos_k, cmp_w1_k, cmp_b1_k, cmp_w2_k, batch, seq_len)
    vc = _compress(vc_raw, cmp_pos_v, cmp_w1_v, cmp_b1_v, cmp_w2_v, batch, seq_len)
    y_nsa = _nsa_attention(qn, qr, kc, vc, ks, vs, kw, vw, gl, z, batch, seq_len)
    y_moba = _moba_attention(mq, kmean, mk, mv, mz, batch, seq_len)
    out = _out_project(x2, y_nsa.reshape(batch * seq_len, D_NSA), y_moba.reshape(batch * seq_len, D_MOBA),
                       w_out, ln_gain, ln_bias)
    return out.reshape(batch, seq_len, D_MODEL)


def kernel(x, w_in, cmp_pos_k, cmp_w1_k, cmp_b1_k, cmp_w2_k, cmp_pos_v, cmp_w1_v, cmp_b1_v, cmp_w2_v,
           w_out, ln_gain, ln_bias):
    tables = _rope_tables(x.shape[1])
    h = x
    for layer in range(w_in.shape[0]):
        h = _layer(h, w_in[layer], cmp_pos_k[layer], cmp_w1_k[layer], cmp_b1_k[layer], cmp_w2_k[layer],
                   cmp_pos_v[layer], cmp_w1_v[layer], cmp_b1_v[layer], cmp_w2_v[layer],
                   w_out[layer], ln_gain[layer], ln_bias[layer], tables)
    return h
```
